```python
import math
import jax
import jax.numpy as jnp
from jax import lax
import numpy as np

D_MODEL = 1024
BATCH = 8
SEQ = 2048
DEPTH = 4
DEC_BATCH = 128
DEC_SEQ = 8
PAST_LEN = 2048
PAGE_SIZE = 128

HEAD_DIM = 64
SB_WIDTH = D_MODEL // 4
SB_HEADS = SB_WIDTH // HEAD_DIM
POOL_WINDOWS = (2, 4, 8, 16)
POOL_GROUPS = len(POOL_WINDOWS)
POOL_WIDTH = D_MODEL // 4
POOL_GDIM = POOL_WIDTH // POOL_GROUPS
POOL_BUF = max(POOL_WINDOWS) - 1
NSA_WIDTH = D_MODEL // 2
NSA_HEADS = NSA_WIDTH // HEAD_DIM
NSA_KV_HEADS = NSA_HEADS // 4
NSA_GROUP = NSA_HEADS // NSA_KV_HEADS
NSA_KV_WIDTH = NSA_KV_HEADS * HEAD_DIM
CMP_BLOCK = 32
SEL_BLOCK = 64
SEL_TOPN = 8
WINDOW = 512
MIX_WIDTH = SB_WIDTH + POOL_WIDTH + NSA_WIDTH
IN_SIZES = (SB_WIDTH,) * 4 + (POOL_WIDTH,) * 2 + (NSA_WIDTH,) + (NSA_KV_WIDTH,) * 6 + (3 * NSA_HEADS, NSA_WIDTH)
IN_WIDTH = sum(IN_SIZES)
ROPE_THETA = 500000.0
ROT_DIM = HEAD_DIM // 4
Q_BLOCK = 128
NORM_EPS = 1e-6
FORCED_SCORE = 1e4
FAR_POS = 1 << 30

kernel_name = 'hybrid_stickbreak_pool_nsa_decoder_step'


def rms_norm(x, g):
    xf = x.astype(jnp.float32)
    y = xf * lax.rsqrt(jnp.mean(xf * xf, axis=-1, keepdims=True) + NORM_EPS)
    return (y * g.astype(jnp.float32)).astype(x.dtype)


def partial_rope(x, pos):
    half = ROT_DIM // 2
    inv = ROPE_THETA ** (-jnp.arange(half, dtype=jnp.float32) * (2.0 / ROT_DIM))
    ang = pos.astype(jnp.float32)[:, None] * inv[None, :]
    cos = jnp.cos(ang)[None, :, None, :]
    sin = jnp.sin(ang)[None, :, None, :]
    xf = x.astype(jnp.float32)
    x1, x2 = xf[..., :half], xf[..., half:ROT_DIM]
    out = jnp.concatenate([x1 * cos - x2 * sin, x2 * cos + x1 * sin, xf[..., ROT_DIM:]], axis=-1)
    return out.astype(x.dtype)


def masked_softmax(s, mask):
    s = jnp.where(mask, s.astype(jnp.float32), -jnp.inf)
    m = jnp.max(s, axis=-1, keepdims=True)
    m = jnp.where(jnp.isfinite(m), m, 0.0)
    p = jnp.exp(s - m)
    return p / jnp.maximum(jnp.sum(p, axis=-1, keepdims=True), 1e-30)


def over_query_blocks(fn, *arrays):
    t = arrays[0].shape[1]
    qb = math.gcd(t, Q_BLOCK)
    nb = t // qb
    split = lambda a: jnp.moveaxis(a.reshape(a.shape[0], nb, qb, *a.shape[2:]), 1, 0)
    out = lax.map(lambda xs: fn(*xs), tuple(split(a) for a in arrays))
    out = jnp.moveaxis(out, 0, 1)
    return out.reshape(out.shape[0], t, *out.shape[3:])


def stick_breaking_attention(q, k, v, q_pos, k_pos):
    scale = HEAD_DIM ** -0.5

    def block(q_b, qp_b):
        qp = qp_b[0]
        z = jnp.einsum('bqhd,bkhd->bhqk', q_b, k).astype(jnp.float32) * scale
        mask = k_pos[None, :] < qp[:, None]
        log_keep = jnp.where(mask, -jax.nn.softplus(z), 0.0)
        later = lax.cumsum(log_keep, axis=3, reverse=True) - log_keep
        a = jnp.where(mask, jnp.exp(jax.nn.log_sigmoid(z) + later), 0.0)
        return jnp.einsum('bhqk,bkhd->bqhd', a.astype(v.dtype), v)

    return over_query_blocks(block, q, q_pos[None])


def pool_mixer(u, prefix, q_pos, w_grp, scale):
    b, t, _ = u.shape
    rows = jnp.concatenate([prefix.astype(u.dtype), u], axis=1)
    cs = jnp.pad(jnp.cumsum(rows.astype(jnp.float32), axis=1), ((0, 0), (1, 0), (0, 0)))
    upto = cs[:, POOL_BUF + 1:]
    uf = u.astype(jnp.float32)
    diffs = []
    for g, w in enumerate(POOL_WINDOWS):
        c = slice(g * POOL_GDIM, (g + 1) * POOL_GDIM)
        before = cs[:, POOL_BUF + 1 - w: POOL_BUF + 1 - w + t, c]
        count = jnp.minimum(q_pos + 1, w).astype(jnp.float32)[None, :, None]
        diffs.append((upto[..., c] - before) / count - uf[..., c])
    d = jnp.stack(diffs, axis=2)
    y = jnp.einsum('btgc,gce->btge', d, w_grp.astype(jnp.float32)).reshape(b, t, POOL_WIDTH)
    y = y * scale.astype(jnp.float32)
    return y.astype(u.dtype), rows[:, -POOL_BUF:]


def nsa_attention(q_raw, q_rot, branch_gate, q_pos, nsa_kv, win_kv, win_pos, phi_k, phi_v):
    b, tq = q_raw.shape[:2]
    gk, dh = NSA_KV_HEADS, HEAD_DIM
    tk = nsa_kv.shape[1]
    tk_pad = -(-tk // SEL_BLOCK) * SEL_BLOCK
    n_pad = tk_pad - tk
    kv = jnp.pad(nsa_kv, ((0, 0), (0, n_pad), (0, 0), (0, 0), (0, 0)))
    k_pos = jnp.concatenate([jnp.arange(tk, dtype=jnp.int32), jnp.full((n_pad,), FAR_POS, jnp.int32)])
    n_cmp = tk_pad // CMP_BLOCK
    cmp_rows = kv[:, :, :2].reshape(b, n_cmp, CMP_BLOCK, 2, gk, dh)
    k_cmp = jnp.einsum('bnlgd,l->bngd', cmp_rows[:, :, :, 0], phi_k)
    v_cmp = jnp.einsum('bnlgd,l->bngd', cmp_rows[:, :, :, 1], phi_v)
    cmp_end = k_pos.reshape(n_cmp, CMP_BLOCK)[:, -1]
    n_sel = tk_pad // SEL_BLOCK
    sel_rows = kv[:, :, 2:].reshape(b, n_sel, SEL_BLOCK, 2, gk, dh)
    k_blk = jnp.transpose(sel_rows[:, :, :, 0], (0, 3, 1, 2, 4))
    v_blk = jnp.transpose(sel_rows[:, :, :, 1], (0, 3, 1, 2, 4))
    sel_pos = k_pos.reshape(n_sel, SEL_BLOCK)
    top_n = min(SEL_TOPN, n_sel)
    blk_id = jnp.arange(n_sel, dtype=jnp.int32)
    win_rows = jnp.pad(win_kv, ((0, 0), (WINDOW, 0), (0, 0), (0, 0), (0, 0)))
    win_rows_pos = jnp.concatenate([jnp.full((WINDOW,), -FAR_POS, jnp.int32), win_pos])
    w_off = win_kv.shape[1] - tq
    scale = HEAD_DIM ** -0.5
    gather_blocks = jax.vmap(jax.vmap(lambda blocks, idx: blocks[idx]))

    def block(qr_b, qc_b, gate_b, qp_b, qi_b):
        qp, i0 = qp_b[0], qi_b[0, 0]
        qb = qp.shape[0]
        qc = qc_b.reshape(b, qb, gk, NSA_GROUP, dh)
        qr = qr_b.reshape(b, qb, gk, NSA_GROUP, dh)
        s_c = jnp.einsum('bqghd,bngd->bghqn', qc, k_cmp) * scale
        p_c = masked_softmax(s_c, cmp_end[None, :] <= qp[:, None])
        o_c = jnp.einsum('bghqn,bngd->bqghd', p_c.astype(v_cmp.dtype), v_cmp)
        imp = p_c.sum(axis=2).reshape(b, gk, qb, n_sel, SEL_BLOCK // CMP_BLOCK).sum(axis=-1)
        cur = qp // SEL_BLOCK
        forced = (blk_id[None, :] == cur[:, None]) | (blk_id[None, :] == 0)
        future = blk_id[None, :] * SEL_BLOCK > qp[:, None]
        imp = jnp.where(forced, FORCED_SCORE, jnp.where(future, -1.0, imp))
        _, sel = lax.top_k(imp, top_n)
        k_sel = gather_blocks(k_blk, sel).reshape(b, gk, qb, top_n * SEL_BLOCK, dh)
        v_sel = gather_blocks(v_blk, sel).reshape(b, gk, qb, top_n * SEL_BLOCK, dh)
        pos_sel = sel_pos[sel].reshape(b, gk, qb, top_n * SEL_BLOCK)
        s_s = jnp.einsum('bqghd,bgqnd->bghqn', qr, k_sel) * scale
        p_s = masked_softmax(s_s, (pos_sel <= qp[:, None])[:, :, None])
        o_s = jnp.einsum('bghqn,bgqnd->bqghd', p_s.astype(v_sel.dtype), v_sel)
        span = WINDOW + qb - 1
        start = i0 + w_off + 1
        rows = lax.dynamic_slice_in_dim(win_rows, start, span, axis=1)
        rows_pos = lax.dynamic_slice_in_dim(win_rows_pos, start, span)
        dist = qp[:, None] - rows_pos[None, :]
        s_w = jnp.einsum('bqghd,bkgd->bghqk', qr, rows[:, :, 0]) * scale
        p_w = masked_softmax(s_w, (dist >= 0) & (dist < WINDOW))
        o_w = jnp.einsum('bghqk,bkgd->bqghd', p_w.astype(rows.dtype), rows[:, :, 1])
        gate = jax.nn.sigmoid(gate_b.astype(jnp.float32)).reshape(b, qb, gk, NSA_GROUP, 3)
        o = gate[..., 0:1] * o_c + gate[..., 1:2] * o_s + gate[..., 2:3] * o_w
        return o.reshape(b, qb, NSA_HEADS, dh).astype(q_raw.dtype)

    qidx = jnp.arange(tq, dtype=jnp.int32)
    return over_query_blocks(block, q_rot, q_raw, branch_gate, q_pos[None], qidx[None])


def trunk_layer(x, pos0, sb_past, nsa_past, win_buf, pool_buf, win_keep,
                g_pre, w_in, pool_w, pool_scale, phi_k, phi_v, w_out, g_post):
    b, t, _ = x.shape
    q_pos = pos0 + jnp.arange(t, dtype=jnp.int32)
    h = rms_norm(x, g_pre)
    proj = jnp.einsum('btd,de->bte', h, w_in)
    offsets = np.cumsum(IN_SIZES)[:-1].tolist()
    (sb_q, sb_k, sb_v, sb_g, pool_u, pool_g, nq, ck, cv, sk, sv, wk, wv,
     bgate, ngate) = jnp.split(proj, offsets, axis=-1)
    heads = lambda a, n: a.reshape(b, t, n, HEAD_DIM)
    sb_new = jnp.stack([heads(sb_k, SB_HEADS), heads(sb_v, SB_HEADS)], axis=2)
    sb_all = jnp.concatenate([sb_past.astype(sb_new.dtype), sb_new], axis=1)
    k_pos = jnp.arange(sb_all.shape[1], dtype=jnp.int32)
    o_a = stick_breaking_attention(heads(sb_q, SB_HEADS), sb_all[:, :, 0], sb_all[:, :, 1], q_pos, k_pos)
    o_a = o_a.reshape(b, t, SB_WIDTH) * jax.nn.silu(sb_g)
    o_b, pool_new = pool_mixer(pool_u, pool_buf, q_pos, pool_w, pool_scale)
    o_b = o_b * jax.nn.silu(pool_g)
    q_raw = heads(nq, NSA_HEADS)
    q_rot = partial_rope(q_raw, q_pos)
    nsa_new = jnp.stack([heads(ck, NSA_KV_HEADS), heads(cv, NSA_KV_HEADS),
                         partial_rope(heads(sk, NSA_KV_HEADS), q_pos), heads(sv, NSA_KV_HEADS)], axis=2)
    nsa_all = jnp.concatenate([nsa_past.astype(nsa_new.dtype), nsa_new], axis=1)
    win_new = jnp.stack([partial_rope(heads(wk, NSA_KV_HEADS), q_pos), heads(wv, NSA_KV_HEADS)], axis=2)
    win_all = jnp.concatenate([win_buf.astype(win_new.dtype), win_new], axis=1)
    win_pos = pos0 - win_buf.shape[1] + jnp.arange(win_all.shape[1], dtype=jnp.int32)
    o_c = nsa_attention(q_raw, q_rot, bgate, q_pos, nsa_all, win_all, win_pos, phi_k, phi_v)
    o_c = o_c.reshape(b, t, NSA_WIDTH) * jax.nn.silu(ngate)
    mixed = jnp.concatenate([o_a, o_b.astype(o_a.dtype), o_c.astype(o_a.dtype)], axis=-1)
    y = x + rms_norm(jnp.einsum('bte,ed->btd', mixed, w_out), g_post).astype(x.dtype)
    return y, sb_new, nsa_new, win_all[:, -win_keep:], pool_new


def setup_inputs(seed: int = 0) -> dict:
    key = jax.random.key(seed)
    ks = jax.random.split(key, 16)
    nrm = jax.random.normal
    n_pages = PAST_LEN // PAGE_SIZE
    n_used = DEC_BATCH * n_pages
    n_pool = n_used + n_used // 4
    win_len = min(WINDOW, PAST_LEN)
    x_prompt = nrm(ks[0], (BATCH, SEQ, D_MODEL), jnp.float32)
    x_sample = nrm(ks[1], (DEC_BATCH, DEC_SEQ, D_MODEL), jnp.float32)
    cache_sb_kv = nrm(ks[2], (DEPTH, n_pool, PAGE_SIZE, 2, SB_HEADS, HEAD_DIM), jnp.float32)
    cache_nsa_kv = nrm(ks[3], (DEPTH, n_pool, PAGE_SIZE, 4, NSA_KV_HEADS, HEAD_DIM), jnp.float32)
    state_win_kv = nrm(ks[4], (DEPTH, DEC_BATCH, win_len, 2, NSA_KV_HEADS, HEAD_DIM), jnp.float32)
    state_pool = nrm(ks[5], (DEPTH, DEC_BATCH, POOL_BUF, POOL_WIDTH), jnp.float32)
    page_table = jax.random.permutation(ks[6], n_pool)[:n_used].reshape(DEC_BATCH, n_pages).astype(jnp.int32)
    norm_pre = 1.0 + 0.05 * nrm(ks[7], (DEPTH, D_MODEL), jnp.float32)
    w_in = nrm(ks[8], (DEPTH, D_MODEL, IN_WIDTH), jnp.float32) * D_MODEL ** -0.5
    pool_w = nrm(ks[9], (DEPTH, POOL_GROUPS, POOL_GDIM, POOL_GDIM), jnp.float32) * POOL_GDIM ** -0.5
    pool_scale = 1.0 + 0.05 * nrm(ks[10], (DEPTH, POOL_WIDTH), jnp.float32)
    phi_k = (1.0 + 0.1 * nrm(ks[11], (DEPTH, CMP_BLOCK), jnp.float32)) / CMP_BLOCK
    phi_v = (1.0 + 0.1 * nrm(ks[12], (DEPTH, CMP_BLOCK), jnp.float32)) / CMP_BLOCK
    w_out = nrm(ks[13], (DEPTH, MIX_WIDTH, D_MODEL), jnp.float32) * MIX_WIDTH ** -0.5
    norm_post = 1.0 + 0.05 * nrm(ks[14], (DEPTH, D_MODEL), jnp.float32)
    return {'x_prompt': x_prompt, 'x_sample': x_sample, 'cache_sb_kv': cache_sb_kv,
            'cache_nsa_kv': cache_nsa_kv, 'state_win_kv': state_win_kv, 'state_pool': state_pool,
            'page_table': page_table, 'norm_pre': norm_pre, 'w_in': w_in, 'pool_w': pool_w,
            'pool_scale': pool_scale, 'phi_k': phi_k, 'phi_v': phi_v, 'w_out': w_out,
            'norm_post': norm_post}


def reference(x_prompt, x_sample, cache_sb_kv, cache_nsa_kv, state_win_kv, state_pool,
              page_table, norm_pre, w_in, pool_w, pool_scale, phi_k, phi_v, w_out, norm_post):
    bp, tp = x_prompt.shape[:2]
    bs = x_sample.shape[0]
    past_len = page_table.shape[1] * cache_sb_kv.shape[2]
    dt = x_prompt.dtype
    prompt_keep = min(WINDOW, tp)
    sample_keep = state_win_kv.shape[2]
    yp, ys = x_prompt, x_sample
    sb_p, nsa_p, win_p, pool_p = [], [], [], []
    sb_s, nsa_s, win_s, pool_s = [], [], [], []
    for layer in range(DEPTH):
        weights = (norm_pre[layer], w_in[layer], pool_w[layer], pool_scale[layer],
                   phi_k[layer], phi_v[layer], w_out[layer], norm_post[layer])
        yp, a, b_, c, d = trunk_layer(
            yp, 0,
            jnp.zeros((bp, 0, 2, SB_HEADS, HEAD_DIM), dt),
            jnp.zeros((bp, 0, 4, NSA_KV_HEADS, HEAD_DIM), dt),
            jnp.zeros((bp, 0, 2, NSA_KV_HEADS, HEAD_DIM), dt),
            jnp.zeros((bp, POOL_BUF, POOL_WIDTH), dt),
            prompt_keep, *weights)
        sb_p.append(a)
        nsa_p.append(b_)
        win_p.append(c)
        pool_p.append(d)
        sb_past = cache_sb_kv[layer][page_table].reshape(bs, past_len, 2, SB_HEADS, HEAD_DIM)
        nsa_past = cache_nsa_kv[layer][page_table].reshape(bs, past_len, 4, NSA_KV_HEADS, HEAD_DIM)
        ys, e, f, g, h = trunk_layer(ys, past_len, sb_past, nsa_past, state_win_kv[layer],
                                     state_pool[layer], sample_keep, *weights)
        sb_s.append(e)
        nsa_s.append(f)
        win_s.append(g)
        pool_s.append(h)
    return (yp, ys, jnp.stack(sb_p), jnp.stack(nsa_p), jnp.stack(win_p), jnp.stack(pool_p),
            jnp.stack(sb_s), jnp.stack(nsa_s), jnp.stack(win_s), jnp.stack(pool_s))
```

```python
import functools

import jax
import jax.numpy as jnp
import numpy as np
from jax import lax
from jax.experimental import pallas as pl
from jax.experimental.pallas import tpu as pltpu

F32 = jnp.float32
BF16 = jnp.bfloat16

D_MODEL = 1024
HEAD_DIM = 64
SB_WIDTH = 256
SB_HEADS = 4
POOL_WINDOWS = (2, 4, 8, 16)
POOL_WIDTH = 256
POOL_GDIM = 64
POOL_BUF = 15
NSA_WIDTH = 512
NSA_HEADS = 8
NSA_KV_HEADS = 2
NSA_GROUP = 4
NSA_KV_WIDTH = 128
CMP_BLOCK = 32
SEL_BLOCK = 64
SEL_TOPN = 8
WINDOW = 512
ROPE_THETA = 500000.0
ROT_DIM = 16
NORM_EPS = 1e-6
FORCED_SCORE = 1e4
SCALE = HEAD_DIM ** -0.5
NEG_INF = float("-inf")

C_SBQ, C_SBK, C_SBG = 0, 256, 768
C_PU, C_PG = 1024, 1280
C_NQ = 1536
C_NSA = 2048
C_WIN = 2560
C_NG = 2816
C_BG = 3328
IN_WIDTH_P = 3456
LANES = 128
ROW_TILE = 512
Q_TILE = 128
VMEM_LIMIT = 56 * 1024 * 1024


def _dot(a, b):
    return jnp.dot(a, b, preferred_element_type=F32)


def _dot_nt(a, b):
    return lax.dot_general(a, b, (((1,), (1,)), ((), ())), preferred_element_type=F32)


def _silu(g):
    return g / (1.0 + jnp.exp(-g))


def _softplus(z):
    return jnp.maximum(z, 0.0) + jnp.log(1.0 + jnp.exp(-jnp.abs(z)))


def _suffix_sum_exclusive(x, u_bf16):
    hi = x.astype(BF16)
    lo = (x - hi.astype(F32)).astype(BF16)
    n = x.shape[0]
    both = _dot(jnp.concatenate([hi, lo], axis=0), u_bf16)
    return both[:n] + both[n:]


def _later_matrix(n):
    r = lax.broadcasted_iota(jnp.int32, (n, n), 0)
    c = lax.broadcasted_iota(jnp.int32, (n, n), 1)
    return (r > c).astype(BF16)


def _in_proj_kernel(phik_ref, phiv_ref, x_ref, g_ref, w_ref, cos_ref, sin_ref,
                    sbq_ref, sbkv_ref, sbg_ref, pu_ref, pg_ref, nqc_ref, nqr_ref,
                    nsa_ref, win_ref, ng_ref, bg_ref, cmp_ref, ck_ref, cv_ref):
    x = x_ref[...]
    ms = jnp.mean(x * x, axis=-1, keepdims=True)
    h = (x * lax.rsqrt(ms + NORM_EPS) * g_ref[...]).astype(BF16)

    def proj(a, b):
        return _dot(h, w_ref[:, a:b])

    cos = cos_ref[...]
    sin = sin_ref[...]
    first = (lax.broadcasted_iota(jnp.int32, cos.shape, 1) % HEAD_DIM) < (ROT_DIM // 2)

    def rope(v):
        swapped = jnp.where(first, pltpu.roll(v, LANES - ROT_DIM // 2, 1), pltpu.roll(v, ROT_DIM // 2, 1))
        return v * cos + swapped * sin

    sbq_ref[...] = proj(C_SBQ, C_SBQ + 256)
    sbkv_ref[...] = proj(C_SBK, C_SBK + 512)
    sbg_ref[...] = proj(C_SBG, C_SBG + 256)
    pu_ref[...] = proj(C_PU, C_PU + 256)
    pg_ref[...] = proj(C_PG, C_PG + 256)
    ng_ref[...] = proj(C_NG, C_NG + 512)
    bg_ref[...] = proj(C_BG, C_BG + 128)
    for c in range(4):
        q = proj(C_NQ + c * LANES, C_NQ + (c + 1) * LANES)
        nqc_ref[:, c * LANES:(c + 1) * LANES] = q
        nqr_ref[:, c * LANES:(c + 1) * LANES] = rope(q)
    ck_ref[...] = proj(C_NSA, C_NSA + 128)
    cv_ref[...] = proj(C_NSA + 128, C_NSA + 256)
    nsa_ref[:, 0:128] = ck_ref[...]
    nsa_ref[:, 128:256] = cv_ref[...]
    nsa_ref[:, 256:384] = rope(proj(C_NSA + 256, C_NSA + 384))
    nsa_ref[:, 384:512] = proj(C_NSA + 384, C_NSA + 512)
    win_ref[:, 0:128] = rope(proj(C_WIN, C_WIN + 128))
    win_ref[:, 128:256] = proj(C_WIN + 128, C_WIN + 256)

    nb = cmp_ref.shape[0]
    for part, (src_ref, phi_ref) in enumerate(((ck_ref, phik_ref), (cv_ref, phiv_ref))):
        acc_e = jnp.zeros((nb, LANES), F32)
        acc_o = jnp.zeros((nb, LANES), F32)
        for l in range(CMP_BLOCK):
            acc_e = acc_e + src_ref[pl.ds(l, nb, stride=SEL_BLOCK), :] * phi_ref[l]
            acc_o = acc_o + src_ref[pl.ds(CMP_BLOCK + l, nb, stride=SEL_BLOCK), :] * phi_ref[l]
        cmp_ref[:, part * 256:part * 256 + 128] = acc_e
        cmp_ref[:, part * 256 + 128:part * 256 + 256] = acc_o


def _in_proj(x, g_pre, w_in_p, phi_k, phi_v, cos_t, sin_t):
    n = x.shape[0]
    tm = ROW_TILE
    row = lambda w: pl.BlockSpec((tm, w), lambda i: (i, 0))
    fixed = lambda s: pl.BlockSpec(s, lambda i: (0, 0))
    smem = pl.BlockSpec(memory_space=pltpu.SMEM)
    widths = (256, 512, 256, 256, 256, 512, 512, 512, 256, 512, 128)
    out_shape = [jax.ShapeDtypeStruct((n, w), F32) for w in widths]
    out_shape.append(jax.ShapeDtypeStruct((n // SEL_BLOCK, 512), F32))
    out_specs = [row(w) for w in widths] + [pl.BlockSpec((tm // SEL_BLOCK, 512), lambda i: (i, 0))]
    return pl.pallas_call(
        _in_proj_kernel,
        grid=(n // tm,),
        in_specs=[smem, smem, row(D_MODEL), fixed((1, D_MODEL)), fixed((D_MODEL, IN_WIDTH_P)),
                  row(LANES), row(LANES)],
        out_specs=out_specs,
        out_shape=out_shape,
        scratch_shapes=[pltpu.VMEM((tm, LANES), F32), pltpu.VMEM((tm, LANES), F32)],
        compiler_params=pltpu.CompilerParams(dimension_semantics=("parallel",), vmem_limit_bytes=VMEM_LIMIT),
        name="in_proj",
    )(phi_k, phi_v, x, g_pre, w_in_p, cos_t, sin_t)


def _stack_sb_heads(q):
    head = lax.broadcasted_iota(jnp.int32, q.shape, 1) // HEAD_DIM
    return jnp.concatenate([jnp.where(head == h, q, 0.0) for h in range(SB_HEADS)], axis=0).astype(BF16)


def _unstack_sb_heads(acc, t):
    head = lax.broadcasted_iota(jnp.int32, (t, SB_WIDTH), 1) // HEAD_DIM
    out = jnp.zeros((t, SB_WIDTH), F32)
    for h in range(SB_HEADS):
        out = jnp.where(head == h, acc[h * t:(h + 1) * t], out)
    return out


def _sb_prompt_kernel(q_ref, k_ref, v_ref, g_ref, o_ref, acc_ref, run_ref):
    tq = q_ref.shape[0]
    tk = tq
    i = pl.program_id(1)
    rows = SB_HEADS * tq
    q4 = _stack_sb_heads(q_ref[...] * SCALE)
    u = _later_matrix(tk)
    qpos = i * tq + lax.broadcasted_iota(jnp.int32, (rows, tk), 0) % tq
    kcol = lax.broadcasted_iota(jnp.int32, (rows, tk), 1)
    acc_ref[...] = jnp.zeros_like(acc_ref)
    run_ref[...] = jnp.zeros_like(run_ref)

    def body(jj, carry):
        j = i - jj
        start = pl.multiple_of(j * tk, tk)
        k = k_ref[pl.ds(start, tk), :].astype(BF16)
        v = v_ref[pl.ds(start, tk), :].astype(BF16)
        z = _dot_nt(q4, k)
        mask = (kcol + j * tk) < qpos
        log_keep = jnp.where(mask, -_softplus(z), 0.0)
        later = _suffix_sum_exclusive(log_keep, u) + run_ref[...]
        a = jnp.where(mask, jnp.exp(z + log_keep + later), 0.0)
        acc_ref[...] += _dot(a.astype(BF16), v)
        run_ref[...] += jnp.sum(log_keep, axis=1, keepdims=True)
        return carry

    lax.fori_loop(0, i + 1, body, 0)
    o_ref[...] = _unstack_sb_heads(acc_ref[...], tq) * _silu(g_ref[...])


def _sb_prompt(sbq, sbkv, sbg, batch, seq):
    tq = Q_TILE
    nq = seq // tq
    n = sbq.shape[0]
    return pl.pallas_call(
        _sb_prompt_kernel,
        grid=(batch, nq),
        in_specs=[pl.BlockSpec((tq, 256), lambda b, i: (b * nq + i, 0)),
                  pl.BlockSpec((seq, 256), lambda b, i: (b, 0)),
                  pl.BlockSpec((seq, 256), lambda b, i: (b, 1)),
                  pl.BlockSpec((tq, 256), lambda b, i: (b * nq + i, 0))],
        out_specs=pl.BlockSpec((tq, 256), lambda b, i: (b * nq + i, 0)),
        out_shape=jax.ShapeDtypeStruct((batch * seq, 256), F32),
        scratch_shapes=[pltpu.VMEM((SB_HEADS * tq, 256), F32), pltpu.VMEM((SB_HEADS * tq, 1), F32)],
        compiler_params=pltpu.CompilerParams(dimension_semantics=("parallel", "parallel"),
                                             vmem_limit_bytes=VMEM_LIMIT),
        name="sb_prompt",
    )(sbq, sbkv, sbkv, sbg)


def _sb_sample_kernel(pt_ref, q_ref, kvn_ref, g_ref, *rest, n_pages, page):
    pages = rest[:n_pages]
    o_ref = rest[n_pages]
    new_ref = rest[n_pages + 1]
    t = q_ref.shape[0]
    rows = SB_HEADS * t
    q4 = _stack_sb_heads(q_ref[...] * SCALE)
    u = _later_matrix(page)
    trow = lax.broadcasted_iota(jnp.int32, (rows, page), 0) % t
    scol = lax.broadcasted_iota(jnp.int32, (rows, page), 1)

    new_ref[...] = jnp.zeros_like(new_ref)
    new_ref[0:t, :] = kvn_ref[...]
    k = new_ref[:, 0:256].astype(BF16)
    v = new_ref[:, 256:512].astype(BF16)
    z = _dot_nt(q4, k)
    mask = scol < trow
    log_keep = jnp.where(mask, -_softplus(z), 0.0)
    later = _suffix_sum_exclusive(log_keep, u)
    a = jnp.where(mask, jnp.exp(z + log_keep + later), 0.0)
    acc = _dot(a.astype(BF16), v)
    run = jnp.sum(log_keep, axis=1, keepdims=True)
    for p in reversed(range(n_pages)):
        k = pages[p][0, 0, :, 0:256].astype(BF16)
        v = pages[p][0, 0, :, 256:512].astype(BF16)
        z = _dot_nt(q4, k)
        log_keep = -_softplus(z)
        later = _suffix_sum_exclusive(log_keep, u) + run
        a = jnp.exp(z + log_keep + later)
        acc = acc + _dot(a.astype(BF16), v)
        run = run + jnp.sum(log_keep, axis=1, keepdims=True)
    o_ref[...] = _unstack_sb_heads(acc, t) * _silu(g_ref[...])


def _page_specs(layer, n_pages, page, width):
    def spec(p):
        return pl.BlockSpec((1, 1, page, width), lambda b, pt: (layer, pt[b, p], 0, 0))
    return [spec(p) for p in range(n_pages)]


def _sb_sample(page_table, sbq, sbkv, sbg, cache, layer, row0, dec_batch, t):
    n_pages = page_table.shape[1]
    page = cache.shape[2]
    blk0 = row0 // t
    new = lambda w: pl.BlockSpec((t, w), lambda b, pt: (blk0 + b, 0))
    grid_spec = pltpu.PrefetchScalarGridSpec(
        num_scalar_prefetch=1,
        grid=(dec_batch,),
        in_specs=[new(256), new(512), new(256)] + _page_specs(layer, n_pages, page, 512),
        out_specs=pl.BlockSpec((t, 256), lambda b, pt: (b, 0)),
        scratch_shapes=[pltpu.VMEM((page, 512), F32)],
    )
    return pl.pallas_call(
        functools.partial(_sb_sample_kernel, n_pages=n_pages, page=page),
        grid_spec=grid_spec,
        out_shape=jax.ShapeDtypeStruct((dec_batch * t, 256), F32),
        compiler_params=pltpu.CompilerParams(dimension_semantics=("parallel",), vmem_limit_bytes=VMEM_LIMIT),
        name="sb_sample",
    )(page_table, sbq, sbkv, sbg, *([cache] * n_pages))


def _pool_kernel(u_ref, g_ref, w_ref, sc_ref, *rest, nb, t, pos0, has_prefix):
    if has_prefix:
        pre_ref, o_ref, rows_ref = rest
    else:
        o_ref, rows_ref = rest
    pad = POOL_BUF + 1
    u = u_ref[...].reshape(nb, t, POOL_WIDTH)
    rows_ref[:, 0:pad, :] = jnp.zeros((nb, pad, POOL_WIDTH), F32)
    if has_prefix:
        rows_ref[:, 1:pad, :] = pre_ref[0]
    rows_ref[:, pad:pad + t, :] = u
    back = lambda s: rows_ref[:, pad - s:pad - s + t, :]
    sums = {}
    total = u
    done = 1
    for w in POOL_WINDOWS:
        for s in range(done, w):
            total = total + back(s)
        done = w
        sums[w] = total
    lane = lax.broadcasted_iota(jnp.int32, (nb, t, POOL_WIDTH), 2) // POOL_GDIM
    pos = pos0 + lax.broadcasted_iota(jnp.int32, (nb, t, POOL_WIDTH), 1)
    window = jnp.zeros((nb, t, POOL_WIDTH), jnp.int32)
    picked = jnp.zeros((nb, t, POOL_WIDTH), F32)
    for gi, w in enumerate(POOL_WINDOWS):
        window = jnp.where(lane == gi, w, window)
        picked = jnp.where(lane == gi, sums[w], picked)
    count = jnp.minimum(pos + 1, window).astype(F32)
    d = picked / count - u
    y = _dot(d.reshape(nb * t, POOL_WIDTH).astype(BF16), w_ref[...]) * sc_ref[...]
    o_ref[...] = y * _silu(g_ref[...])


def _pool(pu, pg, w_bd, scale, prefix, layer, row0, nb, t, n_seq, pos0):
    rows = nb * t
    blk0 = row0 // rows
    has_prefix = prefix is not None
    in_specs = [pl.BlockSpec((rows, 256), lambda i: (blk0 + i, 0)),
                pl.BlockSpec((rows, 256), lambda i: (blk0 + i, 0)),
                pl.BlockSpec((256, 256), lambda i: (0, 0)),
                pl.BlockSpec((1, 256), lambda i: (0, 0))]
    args = [pu, pg, w_bd, scale]
    if has_prefix:
        in_specs.append(pl.BlockSpec((1, nb, POOL_BUF, 256), lambda i: (layer, i, 0, 0)))
        args.append(prefix)
    return pl.pallas_call(
        functools.partial(_pool_kernel, nb=nb, t=t, pos0=pos0, has_prefix=has_prefix),
        grid=(n_seq // nb,),
        in_specs=in_specs,
        out_specs=pl.BlockSpec((rows, 256), lambda i: (i, 0)),
        out_shape=jax.ShapeDtypeStruct((n_seq * t, 256), F32),
        scratch_shapes=[pltpu.VMEM((nb, POOL_BUF + 1 + t, 256), F32)],
        compiler_params=pltpu.CompilerParams(dimension_semantics=("parallel",), vmem_limit_bytes=VMEM_LIMIT),
        name="pool_sample" if has_prefix else "pool_prompt",
    )(*args)


def _stack_nsa_heads(ref, t):
    low = lax.broadcasted_iota(jnp.int32, (t, LANES), 1) < HEAD_DIM
    g0, g1 = [], []
    for c in range(NSA_GROUP):
        ch = ref[:, c * LANES:(c + 1) * LANES] * SCALE
        g0.append(jnp.where(low, ch, 0.0))
        g1.append(jnp.where(low, 0.0, ch))
    return jnp.concatenate(g0 + g1, axis=0).astype(BF16)


def _per_group_rows(x, t):
    a, b = x[:t], x[t:]
    return jnp.concatenate([a] * NSA_GROUP + [b] * NSA_GROUP, axis=0)


def _compressed_branch(qc, kce, kco, vce, vco, qpos, t):
    rows, nblk = qc.shape[0], kce.shape[0]
    s_e = _dot_nt(qc, kce)
    s_o = _dot_nt(qc, kco)
    blk = lax.broadcasted_iota(jnp.int32, (rows, nblk), 1)
    s_e = jnp.where(blk * SEL_BLOCK + (CMP_BLOCK - 1) <= qpos, s_e, NEG_INF)
    s_o = jnp.where(blk * SEL_BLOCK + (SEL_BLOCK - 1) <= qpos, s_o, NEG_INF)
    m = jnp.maximum(jnp.max(s_e, axis=1, keepdims=True), jnp.max(s_o, axis=1, keepdims=True))
    m = jnp.where(m == NEG_INF, 0.0, m)
    p_e = jnp.exp(s_e - m)
    p_o = jnp.exp(s_o - m)
    denom = jnp.maximum(jnp.sum(p_e, axis=1, keepdims=True) + jnp.sum(p_o, axis=1, keepdims=True), 1e-30)
    p_e = p_e / denom
    p_o = p_o / denom
    out = _dot(p_e.astype(BF16), vce) + _dot(p_o.astype(BF16), vco)
    pair = p_e + p_o
    imp = []
    for g in range(NSA_KV_HEADS):
        acc = pair[(g * NSA_GROUP) * t:(g * NSA_GROUP + 1) * t]
        for c in range(1, NSA_GROUP):
            acc = acc + pair[(g * NSA_GROUP + c) * t:(g * NSA_GROUP + c + 1) * t]
        imp.append(acc)
    return out, jnp.concatenate(imp, axis=0)


def _select_blocks(imp, qpos, n_sel):
    blk = lax.broadcasted_iota(jnp.int32, imp.shape, 1)
    forced = (blk == qpos // SEL_BLOCK) | (blk == 0)
    future = blk * SEL_BLOCK > qpos
    imp = jnp.where(forced, FORCED_SCORE, jnp.where(future, -1.0, imp))
    rank = jnp.zeros(imp.shape, jnp.int32)
    for i in range(n_sel):
        col = imp[:, i:i + 1]
        ahead = (col > imp) | ((col == imp) & (blk > i))
        rank = rank + ahead.astype(jnp.int32)
    top_n = min(SEL_TOPN, n_sel)
    return ((rank < top_n) & (blk < n_sel)).astype(BF16)


def _expand_matrix(n_rows, n_keys, key0):
    r = lax.broadcasted_iota(jnp.int32, (n_rows, n_keys), 0)
    c = lax.broadcasted_iota(jnp.int32, (n_rows, n_keys), 1)
    return ((key0 + c) // SEL_BLOCK == r).astype(BF16)


def _mix_and_store(o_ref, ng_ref, bg_ref, o_c, o_s, o_w, t):
    gate = 1.0 / (1.0 + jnp.exp(-bg_ref[...]))
    low = lax.broadcasted_iota(jnp.int32, (t, LANES), 1) < HEAD_DIM
    for c in range(NSA_GROUP):
        per_group = []
        for g in range(NSA_KV_HEADS):
            r0 = (g * NSA_GROUP + c) * t
            col = (g * NSA_GROUP + c) * 3
            per_group.append(gate[:, col:col + 1] * o_c[r0:r0 + t]
                             + gate[:, col + 1:col + 2] * o_s[r0:r0 + t]
                             + gate[:, col + 2:col + 3] * o_w[r0:r0 + t])
        mixed = jnp.where(low, per_group[0], per_group[1])
        sl = slice(c * LANES, (c + 1) * LANES)
        o_ref[:, sl] = mixed * _silu(ng_ref[:, sl])


def _online_softmax_step(s, v, m_ref, l_ref, acc_ref):
    m_old = m_ref[...]
    m_new = jnp.maximum(m_old, jnp.max(s, axis=1, keepdims=True))
    m_safe = jnp.where(m_new == NEG_INF, 0.0, m_new)
    p = jnp.exp(s - m_safe)
    alpha = jnp.exp(m_old - m_safe)
    l_ref[...] = alpha * l_ref[...] + jnp.sum(p, axis=1, keepdims=True)
    acc_ref[...] = alpha * acc_ref[...] + _dot(p.astype(BF16), v)
    m_ref[...] = m_new


def _nsa_prompt_kernel(nqc_ref, nqr_ref, bg_ref, ng_ref, cmp_ref, ks_ref, vs_ref, wk_ref, wv_ref,
                       o_ref, m_ref, l_ref, acc_ref):
    tq = nqc_ref.shape[0]
    tk = tq
    n_sel = cmp_ref.shape[0]
    i = pl.program_id(1)
    rows = NSA_HEADS * tq
    qc = _stack_nsa_heads(nqc_ref, tq)
    qr = _stack_nsa_heads(nqr_ref, tq)
    qpos = i * tq + lax.broadcasted_iota(jnp.int32, (rows, 1), 0) % tq
    qpos2 = i * tq + lax.broadcasted_iota(jnp.int32, (NSA_KV_HEADS * tq, 1), 0) % tq

    o_c, imp = _compressed_branch(qc, cmp_ref[:, 0:128].astype(BF16), cmp_ref[:, 128:256].astype(BF16),
                                  cmp_ref[:, 256:384].astype(BF16), cmp_ref[:, 384:512].astype(BF16), qpos, tq)
    sel = _select_blocks(imp, qpos2, n_sel)

    def reset():
        m_ref[...] = jnp.full_like(m_ref, NEG_INF)
        l_ref[...] = jnp.zeros_like(l_ref)
        acc_ref[...] = jnp.zeros_like(acc_ref)

    def finish():
        return acc_ref[...] / jnp.maximum(l_ref[...], 1e-30)

    kcol = lax.broadcasted_iota(jnp.int32, (NSA_KV_HEADS * tq, tk), 1)

    reset()

    def slc_body(j, carry):
        start = pl.multiple_of(j * tk, tk)
        k = ks_ref[pl.ds(start, tk), :].astype(BF16)
        v = vs_ref[pl.ds(start, tk), :].astype(BF16)
        member = _dot(sel, _expand_matrix(n_sel, tk, j * tk))
        ok = (member > 0.5) & (kcol + j * tk <= qpos2)
        bias = _per_group_rows(jnp.where(ok, 0.0, NEG_INF), tq)
        _online_softmax_step(_dot_nt(qr, k) + bias, v, m_ref, l_ref, acc_ref)
        return carry

    lax.fori_loop(0, i + 1, slc_body, 0)
    o_s = finish()

    reset()

    def win_body(j, carry):
        start = pl.multiple_of(j * tk, tk)
        k = wk_ref[pl.ds(start, tk), :].astype(BF16)
        v = wv_ref[pl.ds(start, tk), :].astype(BF16)
        dist = qpos2 - (kcol + j * tk)
        bias = _per_group_rows(jnp.where((dist >= 0) & (dist < WINDOW), 0.0, NEG_INF), tq)
        _online_softmax_step(_dot_nt(qr, k) + bias, v, m_ref, l_ref, acc_ref)
        return carry

    lax.fori_loop(jnp.maximum(i - WINDOW // tk, 0), i + 1, win_body, 0)
    o_w = finish()

    _mix_and_store(o_ref, ng_ref, bg_ref, o_c, o_s, o_w, tq)


def _nsa_prompt(nqc, nqr, bgate, ngate, cmp, nsa_new, win_new, batch, seq):
    tq = Q_TILE
    nq = seq // tq
    n_sel = seq // SEL_BLOCK
    qrow = lambda w: pl.BlockSpec((tq, w), lambda b, i: (b * nq + i, 0))
    seq_chunk = lambda c: pl.BlockSpec((seq, LANES), lambda b, i: (b, c))
    rows = NSA_HEADS * tq
    return pl.pallas_call(
        _nsa_prompt_kernel,
        grid=(batch, nq),
        in_specs=[qrow(512), qrow(512), qrow(128), qrow(512),
                  pl.BlockSpec((n_sel, 512), lambda b, i: (b, 0)),
                  seq_chunk(2), seq_chunk(3), seq_chunk(0), seq_chunk(1)],
        out_specs=qrow(512),
        out_shape=jax.ShapeDtypeStruct((batch * seq, 512), F32),
        scratch_shapes=[pltpu.VMEM((rows, 1), F32), pltpu.VMEM((rows, 1), F32), pltpu.VMEM((rows, LANES), F32)],
        compiler_params=pltpu.CompilerParams(dimension_semantics=("parallel", "parallel"),
                                             vmem_limit_bytes=VMEM_LIMIT),
        name="nsa_prompt",
    )(nqc, nqr, bgate, ngate, cmp, nsa_new, nsa_new, win_new, win_new)


def _softmax_parts(parts):
    m = None
    for s, _ in parts:
        pm = jnp.max(s, axis=1, keepdims=True)
        m = pm if m is None else jnp.maximum(m, pm)
    m = jnp.where(m == NEG_INF, 0.0, m)
    denom = None
    acc = None
    for s, v in parts:
        p = jnp.exp(s - m)
        ps = jnp.sum(p, axis=1, keepdims=True)
        pv = _dot(p.astype(BF16), v)
        denom = ps if denom is None else denom + ps
        acc = pv if acc is None else acc + pv
    return acc / jnp.maximum(denom, 1e-30)


def _nsa_sample_kernel(pt_ref, phik_ref, phiv_ref, nqc_ref, nqr_ref, new_ref, winn_ref, bg_ref, ng_ref,
                       swin_ref, *rest, n_pages, page, pos0):
    pages = rest[:n_pages]
    o_ref, kcmp_ref, vcmp_ref, ckall_ref, cvall_ref, newpad_ref, winpad_ref = rest[n_pages:]
    t = nqc_ref.shape[0]
    rows = NSA_HEADS * t
    n_past_sel = n_pages * page // SEL_BLOCK
    n_sel = (pos0 + t + SEL_BLOCK - 1) // SEL_BLOCK
    n_lanes = kcmp_ref.shape[0] // 2
    qc = _stack_nsa_heads(nqc_ref, t)
    qr = _stack_nsa_heads(nqr_ref, t)
    trow = lax.broadcasted_iota(jnp.int32, (rows, 1), 0) % t
    trow2 = lax.broadcasted_iota(jnp.int32, (NSA_KV_HEADS * t, 1), 0) % t
    qpos, qpos2 = pos0 + trow, pos0 + trow2

    n_cmp = n_pages * page // CMP_BLOCK
    for p in range(n_pages):
        ckall_ref[p * page:(p + 1) * page, :] = pages[p][0, 0, :, 0:128]
        cvall_ref[p * page:(p + 1) * page, :] = pages[p][0, 0, :, 128:256]
    halves = []
    for src_ref, phi_ref, dst_ref in ((ckall_ref, phik_ref, kcmp_ref), (cvall_ref, phiv_ref, vcmp_ref)):
        acc = jnp.zeros((n_cmp, LANES), F32)
        for l in range(CMP_BLOCK):
            acc = acc + src_ref[pl.ds(l, n_cmp, stride=CMP_BLOCK), :] * phi_ref[l]
        dst_ref[...] = jnp.zeros_like(dst_ref)
        dst_ref[0:n_cmp, :] = acc
        halves.append((dst_ref[pl.ds(0, n_lanes, stride=2), :].astype(BF16),
                       dst_ref[pl.ds(1, n_lanes, stride=2), :].astype(BF16)))
    (kce, kco), (vce, vco) = halves
    o_c, imp = _compressed_branch(qc, kce, kco, vce, vco, qpos, t)
    sel = _select_blocks(imp, qpos2, n_sel)

    newpad_ref[...] = jnp.zeros_like(newpad_ref)
    newpad_ref[0:t, :] = new_ref[...]
    winpad_ref[...] = jnp.zeros_like(winpad_ref)
    winpad_ref[0:t, :] = winn_ref[...]
    scol = lax.broadcasted_iota(jnp.int32, (NSA_KV_HEADS * t, page), 1)

    member = _dot(sel, _expand_matrix(n_lanes, n_pages * page, 0))
    bias_past = _per_group_rows(jnp.where(member > 0.5, 0.0, NEG_INF), t)
    parts = []
    for p in range(n_pages):
        k = pages[p][0, 0, :, 256:384].astype(BF16)
        v = pages[p][0, 0, :, 384:512].astype(BF16)
        parts.append((_dot_nt(qr, k) + bias_past[:, p * page:(p + 1) * page], v))
    member_new = _dot(sel, _expand_matrix(n_lanes, page, n_pages * page))
    ok_new = (member_new > 0.5) & (scol <= trow2) & (scol < t)
    bias_new = _per_group_rows(jnp.where(ok_new, 0.0, NEG_INF), t)
    parts.append((_dot_nt(qr, newpad_ref[:, 256:384].astype(BF16)) + bias_new, newpad_ref[:, 384:512].astype(BF16)))
    o_s = _softmax_parts(parts)

    wlen = swin_ref.shape[2]
    rcol = lax.broadcasted_iota(jnp.int32, (NSA_KV_HEADS * t, wlen), 1)
    dist = wlen + trow2 - rcol
    bias_w = _per_group_rows(jnp.where((dist >= 0) & (dist < WINDOW), 0.0, NEG_INF), t)
    dist_n = trow2 - scol
    bias_wn = _per_group_rows(jnp.where((dist_n >= 0) & (dist_n < WINDOW) & (scol < t), 0.0, NEG_INF), t)
    parts = [(_dot_nt(qr, swin_ref[0, 0, :, 0:128].astype(BF16)) + bias_w, swin_ref[0, 0, :, 128:256].astype(BF16)),
             (_dot_nt(qr, winpad_ref[:, 0:128].astype(BF16)) + bias_wn, winpad_ref[:, 128:256].astype(BF16))]
    o_w = _softmax_parts(parts)

    _mix_and_store(o_ref, ng_ref, bg_ref, o_c, o_s, o_w, t)


def _nsa_sample(page_table, phi_k, phi_v, nqc, nqr, nsa_new, win_new, bgate, ngate, state_win, cache,
                layer, row0, dec_batch, t, pos0):
    n_pages = page_table.shape[1]
    page = cache.shape[2]
    blk0 = row0 // t
    new = lambda w: pl.BlockSpec((t, w), lambda b, pt: (blk0 + b, 0))
    smem = pl.BlockSpec(memory_space=pltpu.SMEM)
    wlen = state_win.shape[2]
    n_cmp_rows = 2 * LANES
    assert n_pages * page // CMP_BLOCK <= n_cmp_rows
    grid_spec = pltpu.PrefetchScalarGridSpec(
        num_scalar_prefetch=1,
        grid=(dec_batch,),
        in_specs=[smem, smem, new(512), new(512), new(512), new(256), new(128), new(512),
                  pl.BlockSpec((1, 1, wlen, 256), lambda b, pt: (layer, b, 0, 0))]
        + _page_specs(layer, n_pages, page, 512),
        out_specs=pl.BlockSpec((t, 512), lambda b, pt: (b, 0)),
        scratch_shapes=[pltpu.VMEM((n_cmp_rows, LANES), F32), pltpu.VMEM((n_cmp_rows, LANES), F32),
                        pltpu.VMEM((n_pages * page, LANES), F32), pltpu.VMEM((n_pages * page, LANES), F32),
                        pltpu.VMEM((page, 512), F32), pltpu.VMEM((page, 256), F32)],
    )
    return pl.pallas_call(
        functools.partial(_nsa_sample_kernel, n_pages=n_pages, page=page, pos0=pos0),
        grid_spec=grid_spec,
        out_shape=jax.ShapeDtypeStruct((dec_batch * t, 512), F32),
        compiler_params=pltpu.CompilerParams(dimension_semantics=("parallel",), vmem_limit_bytes=VMEM_LIMIT),
        name="nsa_sample",
    )(page_table, phi_k, phi_v, nqc, nqr, nsa_new, win_new, bgate, ngate, state_win, *([cache] * n_pages))


def _out_proj_kernel(oa_ref, ob_ref, oc_ref, x_ref, w_ref, g_ref, y_ref):
    out = (_dot(oa_ref[...].astype(BF16), w_ref[0:256, :])
           + _dot(ob_ref[...].astype(BF16), w_ref[256:512, :])
           + _dot(oc_ref[...].astype(BF16), w_ref[512:1024, :]))
    ms = jnp.mean(out * out, axis=-1, keepdims=True)
    y_ref[...] = x_ref[...] + out * lax.rsqrt(ms + NORM_EPS) * g_ref[...]


def _out_proj(oa, ob, oc, x, w_out_p, g_post):
    n = x.shape[0]
    tm = ROW_TILE
    row = lambda w: pl.BlockSpec((tm, w), lambda i: (i, 0))
    fixed = lambda s: pl.BlockSpec(s, lambda i: (0, 0))
    return pl.pallas_call(
        _out_proj_kernel,
        grid=(n // tm,),
        in_specs=[row(256), row(256), row(512), row(D_MODEL), fixed((D_MODEL, D_MODEL)), fixed((1, D_MODEL))],
        out_specs=row(D_MODEL),
        out_shape=jax.ShapeDtypeStruct((n, D_MODEL), F32),
        compiler_params=pltpu.CompilerParams(dimension_semantics=("parallel",), vmem_limit_bytes=VMEM_LIMIT),
        name="out_proj",
    )(oa, ob, oc, x, w_out_p, g_post)


def _head_permutation():
    perm = []
    for c in range(NSA_GROUP):
        for g in range(NSA_KV_HEADS):
            h = g * NSA_GROUP + c
            perm.extend(range(h * HEAD_DIM, (h + 1) * HEAD_DIM))
    return np.asarray(perm, np.int32)


def _rope_tables(pos):
    half = ROT_DIM // 2
    inv = ROPE_THETA ** (-jnp.arange(half, dtype=F32) * (2.0 / ROT_DIM))
    ang = pos.astype(F32)[:, None] * inv[None, :]
    cos, sin = jnp.cos(ang), jnp.sin(ang)
    n = pos.shape[0]
    cos_h = jnp.concatenate([cos, cos, jnp.ones((n, HEAD_DIM - ROT_DIM), F32)], axis=1)
    sin_h = jnp.concatenate([-sin, sin, jnp.zeros((n, HEAD_DIM - ROT_DIM), F32)], axis=1)
    return jnp.tile(cos_h, (1, LANES // HEAD_DIM)), jnp.tile(sin_h, (1, LANES // HEAD_DIM))


def kernel(x_prompt, x_sample, cache_sb_kv, cache_nsa_kv, state_win_kv, state_pool, page_table, norm_pre, w_in,
           pool_w, pool_scale, phi_k, phi_v, w_out, norm_post):
    bp, tp, d = x_prompt.shape
    bs, ts, _ = x_sample.shape
    depth = w_in.shape[0]
    n_pool, page = cache_sb_kv.shape[1], cache_sb_kv.shape[2]
    past_len = page_table.shape[1] * page
    n_p, n_s = bp * tp, bs * ts
    assert d == D_MODEL and n_p % ROW_TILE == 0 and n_s % ROW_TILE == 0
    assert tp % Q_TILE == 0 and tp % SEL_BLOCK == 0 and n_p % n_s == 0
    assert past_len % SEL_BLOCK == 0 and ts < CMP_BLOCK and ts % 8 == 0 and page % SEL_BLOCK == 0
    assert state_win_kv.shape[2] >= WINDOW or state_win_kv.shape[2] == past_len

    perm = _head_permutation()
    o = np.cumsum((0,) + (256,) * 6 + (512,) + (128,) * 6 + (24, 512))
    cols = np.concatenate([np.arange(o[0], o[6]), o[6] + perm, np.arange(o[7], o[13]), o[14] + perm,
                           np.arange(o[13], o[14])])
    w_in_p = jnp.pad(w_in[:, :, cols], ((0, 0), (0, 0), (0, IN_WIDTH_P - cols.size))).astype(BF16)
    rows = np.concatenate([np.arange(512), 512 + perm])
    w_out_p = w_out[:, rows, :].astype(BF16)
    w_pool = jnp.zeros((depth, POOL_WIDTH, POOL_WIDTH), F32)
    for g in range(len(POOL_WINDOWS)):
        sl = slice(g * POOL_GDIM, (g + 1) * POOL_GDIM)
        w_pool = w_pool.at[:, sl, sl].set(pool_w[:, g])
    w_pool = w_pool.astype(BF16)

    pos = jnp.concatenate([jnp.tile(jnp.arange(tp, dtype=jnp.int32), bp),
                           jnp.tile(past_len + jnp.arange(ts, dtype=jnp.int32), bs)])
    cos_t, sin_t = _rope_tables(pos)

    cache_sb = cache_sb_kv.reshape(depth, n_pool, page, 512)
    cache_nsa = cache_nsa_kv.reshape(depth, n_pool, page, 512)
    wlen = state_win_kv.shape[2]
    state_win = state_win_kv.reshape(depth, bs, wlen, 256)

    x = jnp.concatenate([x_prompt.reshape(n_p, d), x_sample.reshape(n_s, d)], axis=0)
    outs = [[] for _ in range(8)]
    for layer in range(depth):
        (sbq, sbkv, sbg, pu, pg, nqc, nqr, nsa_new, win_new, ngate, bgate, cmp) = _in_proj(
            x, norm_pre[layer][None], w_in_p[layer], phi_k[layer], phi_v[layer], cos_t, sin_t)
        oa = jnp.concatenate([
            _sb_prompt(sbq, sbkv, sbg, bp, tp),
            _sb_sample(page_table, sbq, sbkv, sbg, cache_sb, layer, n_p, bs, ts)], axis=0)
        ob = jnp.concatenate([
            _pool(pu, pg, w_pool[layer], pool_scale[layer][None], None, layer, 0, 1, tp, bp, 0),
            _pool(pu, pg, w_pool[layer], pool_scale[layer][None], state_pool, layer, n_p, bs, ts, bs, past_len)],
            axis=0)
        oc = jnp.concatenate([
            _nsa_prompt(nqc, nqr, bgate, ngate, cmp, nsa_new, win_new, bp, tp),
            _nsa_sample(page_table, phi_k[layer], phi_v[layer], nqc, nqr, nsa_new, win_new, bgate, ngate,
                        state_win, cache_nsa, layer, n_p, bs, ts, past_len)], axis=0)
        x = _out_proj(oa, ob, oc, x, w_out_p[layer], norm_post[layer][None])

        keep_p = min(WINDOW, tp)
        win_s = jnp.concatenate([state_win[layer], win_new[n_p:].reshape(bs, ts, 256)], axis=1)[:, -wlen:]
        pool_s = jnp.concatenate([state_pool[layer], pu[n_p:].reshape(bs, ts, 256)], axis=1)[:, -POOL_BUF:]
        outs[0].append(sbkv[:n_p].reshape(bp, tp, 2, SB_HEADS, HEAD_DIM))
        outs[1].append(nsa_new[:n_p].reshape(bp, tp, 4, NSA_KV_HEADS, HEAD_DIM))
        outs[2].append(win_new[:n_p].reshape(bp, tp, 2, NSA_KV_HEADS, HEAD_DIM)[:, tp - keep_p:])
        outs[3].append(pu[:n_p].reshape(bp, tp, POOL_WIDTH)[:, tp - POOL_BUF:])
        outs[4].append(sbkv[n_p:].reshape(bs, ts, 2, SB_HEADS, HEAD_DIM))
        outs[5].append(nsa_new[n_p:].reshape(bs, ts, 4, NSA_KV_HEADS, HEAD_DIM))
        outs[6].append(win_s.reshape(bs, wlen, 2, NSA_KV_HEADS, HEAD_DIM))
        outs[7].append(pool_s)
    y_prompt = x[:n_p].reshape(bp, tp, d)
    y_sample = x[n_p:].reshape(bs, ts, d)
    return (y_prompt, y_sample) + tuple(jnp.stack(o_) for o_ in outs)
```

```python
import functools

import jax
import jax.numpy as jnp
import numpy as np
from jax import lax
from jax.experimental import pallas as pl
from jax.experimental.pallas import tpu as pltpu

F32 = jnp.float32
BF16 = jnp.bfloat16

D_MODEL = 1024
HEAD_DIM = 64
SB_WIDTH = 256
SB_HEADS = 4
POOL_WINDOWS = (2, 4, 8, 16)
POOL_WIDTH = 256
POOL_GDIM = 64
POOL_BUF = 15
NSA_WIDTH = 512
NSA_HEADS = 8
NSA_KV_HEADS = 2
NSA_GROUP = 4
NSA_KV_WIDTH = 128
CMP_BLOCK = 32
SEL_BLOCK = 64
SEL_TOPN = 8
WINDOW = 512
ROPE_THETA = 500000.0
ROT_DIM = 16
ROT_HALF = ROT_DIM // 2
NORM_EPS = 1e-6
FORCED_SCORE = 1e4
SCALE = HEAD_DIM ** -0.5
NEG_INF = float("-inf")
MASKED = -(2.0 ** 100)

C_SBQ, C_SBK, C_SBV, C_SBG = 0, 256, 512, 768
C_PU, C_PG = 1024, 1280
C_NQ = 1536
C_NSA = 2048
C_WIN = 2560
C_NG = 2816
C_BG = 3328
IN_WIDTH_P = 3456
R_SBQ, R_SBV, R_NQ, R_SV, R_WV, R_BG, IN_ROWS_T = 0, 256, 512, 1024, 1152, 1280, 1408
LANES = 128
ROW_TILE = 512
Q_TILE = 128
K_TILE = 256
VMEM_LIMIT = 56 * 1024 * 1024


def _dot(a, b):
    return jnp.dot(a, b, preferred_element_type=F32)


def _dot_nt(a, b):
    return lax.dot_general(a, b, (((1,), (1,)), ((), ())), preferred_element_type=F32)


def _dot_tn(a, b):
    return lax.dot_general(a, b, (((0,), (0,)), ((), ())), preferred_element_type=F32)


def _silu(g):
    return g / (1.0 + jnp.exp(-g))


def _softplus(z):
    return jnp.maximum(z, 0.0) + jnp.log(1.0 + jnp.exp(-jnp.abs(z)))


def _split_bf16(x):
    hi = x.astype(BF16)
    return hi, (x - hi.astype(F32)).astype(BF16)


def _suffix_sum_exclusive(x, u_bf16):
    hi, lo = _split_bf16(x)
    n = x.shape[0]
    both = _dot(jnp.concatenate([hi, lo], axis=0), u_bf16)
    return both[:n] + both[n:]


def _later_matrix(n, transposed=False):
    r = lax.broadcasted_iota(jnp.int32, (n, n), 0)
    c = lax.broadcasted_iota(jnp.int32, (n, n), 1)
    return ((c > r) if transposed else (r > c)).astype(BF16)


def _in_proj_kernel(phik_ref, phiv_ref, x_ref, g_ref, w_ref, wt_ref, cos_ref, sin_ref, cost_ref, sint_ref,
                    sbkv_ref, sbg_ref, pu_ref, pg_ref, nsa_ref, win_ref, ng_ref, cmp_ref,
                    sbq_ref, nqc_ref, nqr_ref, bg_ref,
                    sbqt_ref, sbvt_ref, nqct_ref, nqrt_ref, nsvt_ref, nwvt_ref, bgt_ref,
                    ck_ref, cv_ref, *, n_prompt_tiles):
    i = pl.program_id(0)
    x = x_ref[...]
    ms = jnp.mean(x * x, axis=-1, keepdims=True)
    h = (x * lax.rsqrt(ms + NORM_EPS) * g_ref[...]).astype(BF16)

    def proj(a, b):
        return _dot(h, w_ref[:, a:b])

    cos = cos_ref[...]
    sin = sin_ref[...]
    first = (lax.broadcasted_iota(jnp.int32, cos.shape, 1) % HEAD_DIM) < ROT_HALF

    def rope(v):
        swapped = jnp.where(first, pltpu.roll(v, LANES - ROT_HALF, 1), pltpu.roll(v, ROT_HALF, 1))
        return v * cos + swapped * sin

    sbkv_ref[...] = proj(C_SBK, C_SBK + 512)
    sbg_ref[...] = proj(C_SBG, C_SBG + 256)
    pu_ref[...] = proj(C_PU, C_PU + 256)
    pg_ref[...] = proj(C_PG, C_PG + 256)
    ng_ref[...] = proj(C_NG, C_NG + 512)
    ck_ref[...] = proj(C_NSA, C_NSA + 128)
    cv_ref[...] = proj(C_NSA + 128, C_NSA + 256)
    nsa_ref[:, 0:128] = ck_ref[...]
    nsa_ref[:, 128:256] = cv_ref[...]
    nsa_ref[:, 256:384] = rope(proj(C_NSA + 256, C_NSA + 384))
    nsa_ref[:, 384:512] = proj(C_NSA + 384, C_NSA + 512)
    win_ref[:, 0:128] = rope(proj(C_WIN, C_WIN + 128))
    win_ref[:, 128:256] = proj(C_WIN + 128, C_WIN + 256)

    nb = cmp_ref.shape[0]
    for part, (src_ref, phi_ref) in enumerate(((ck_ref, phik_ref), (cv_ref, phiv_ref))):
        acc_e = jnp.zeros((nb, LANES), F32)
        acc_o = jnp.zeros((nb, LANES), F32)
        for l in range(CMP_BLOCK):
            acc_e = acc_e + src_ref[pl.ds(l, nb, stride=SEL_BLOCK), :] * phi_ref[l]
            acc_o = acc_o + src_ref[pl.ds(CMP_BLOCK + l, nb, stride=SEL_BLOCK), :] * phi_ref[l]
        cmp_ref[:, part * 256:part * 256 + 128] = acc_e
        cmp_ref[:, part * 256 + 128:part * 256 + 256] = acc_o

    @pl.when(i >= n_prompt_tiles)
    def _():
        sbq_ref[...] = proj(C_SBQ, C_SBQ + 256)
        bg_ref[...] = proj(C_BG, C_BG + 128)
        for c in range(NSA_GROUP):
            q = proj(C_NQ + c * LANES, C_NQ + (c + 1) * LANES)
            nqc_ref[:, c * LANES:(c + 1) * LANES] = q
            nqr_ref[:, c * LANES:(c + 1) * LANES] = rope(q)

    @pl.when(i < n_prompt_tiles)
    def _():
        def proj_t(a, b):
            return _dot_nt(wt_ref[a:b, :], h)

        cos_t = cost_ref[...]
        sin_t = sint_ref[...]

        def rope_t(v):
            pieces = []
            for base in range(0, LANES, HEAD_DIM):
                x1 = v[base:base + ROT_HALF]
                x2 = v[base + ROT_HALF:base + ROT_DIM]
                pieces += [x1 * cos_t - x2 * sin_t, x2 * cos_t + x1 * sin_t, v[base + ROT_DIM:base + HEAD_DIM]]
            return jnp.concatenate(pieces, axis=0)

        sbqt_ref[...] = proj_t(R_SBQ, R_SBQ + 256)
        sbvt_ref[...] = proj_t(R_SBV, R_SBV + 256)
        nsvt_ref[...] = proj_t(R_SV, R_SV + 128)
        nwvt_ref[...] = proj_t(R_WV, R_WV + 128)
        bgt_ref[...] = proj_t(R_BG, R_BG + 128)
        for c in range(NSA_GROUP):
            q = proj_t(R_NQ + c * LANES, R_NQ + (c + 1) * LANES)
            nqct_ref[c * LANES:(c + 1) * LANES, :] = q
            nqrt_ref[c * LANES:(c + 1) * LANES, :] = rope_t(q)


def _in_proj(x, g_pre, w_in_p, w_in_t, phi_k, phi_v, tables, n_p):
    n = x.shape[0]
    n_s = n - n_p
    tm = ROW_TILE
    npt = n_p // tm
    cos_r, sin_r, cos_t, sin_t = tables
    row = lambda w: pl.BlockSpec((tm, w), lambda i: (i, 0))
    fixed = lambda s: pl.BlockSpec(s, lambda i: (0, 0), pipeline_mode=pl.Buffered(1))
    smem = pl.BlockSpec(memory_space=pltpu.SMEM)
    sample_row =lambda w: pl.BlockSpec((tm, w), lambda i: (jnp.maximum(i - npt, 0), 0))
    prompt_col = lambda f: pl.BlockSpec((f, tm), lambda i: (0, jnp.minimum(i, npt - 1)))
    all_widths = (512, 256, 256, 256, 512, 256, 512)
    sample_widths = (256, 512, 512, 128)
    prompt_feats = (256, 256, 512, 512, 128, 128, 128)
    out_shape = ([jax.ShapeDtypeStruct((n, w), F32) for w in all_widths]
                 + [jax.ShapeDtypeStruct((n // SEL_BLOCK, 512), F32)]
                 + [jax.ShapeDtypeStruct((n_s, w), F32) for w in sample_widths]
                 + [jax.ShapeDtypeStruct((f, n_p), F32) for f in prompt_feats])
    out_specs = ([row(w) for w in all_widths] + [pl.BlockSpec((tm // SEL_BLOCK, 512), lambda i: (i, 0))]
                 + [sample_row(w) for w in sample_widths] + [prompt_col(f) for f in prompt_feats])
    return pl.pallas_call(
        functools.partial(_in_proj_kernel, n_prompt_tiles=npt),
        grid=(n // tm,),
        in_specs=[smem, smem, row(D_MODEL), fixed((1, D_MODEL)), fixed((D_MODEL, IN_WIDTH_P)),
                  fixed((IN_ROWS_T, D_MODEL)), row(LANES), row(LANES),
                  pl.BlockSpec((ROT_HALF, tm), lambda i: (0, i)), pl.BlockSpec((ROT_HALF, tm), lambda i: (0, i))],
        out_specs=out_specs,
        out_shape=out_shape,
        scratch_shapes=[pltpu.VMEM((tm, LANES), F32), pltpu.VMEM((tm, LANES), F32)],
        compiler_params=pltpu.CompilerParams(dimension_semantics=("arbitrary",), vmem_limit_bytes=VMEM_LIMIT),
        name="in_proj",
    )(phi_k, phi_v, x, g_pre, w_in_p, w_in_t, cos_r, sin_r, cos_t, sin_t)


def _sb_prompt_kernel(qt_ref, k_ref, vt_ref, g_ref, o_ref, acc_ref, run_ref):
    tq = qt_ref.shape[1]
    tk = K_TILE
    i = pl.program_id(1)
    q0 = i * tq
    n_rows = SB_HEADS * tq
    qt = qt_ref[...] * SCALE
    head = lax.broadcasted_iota(jnp.int32, qt.shape, 0) // HEAD_DIM
    q4t = jnp.concatenate([jnp.where(head == h, qt, 0.0) for h in range(SB_HEADS)], axis=1).astype(BF16)
    u_t = _later_matrix(tk, transposed=True)
    qpos = q0 + lax.broadcasted_iota(jnp.int32, (tk, n_rows), 1) % tq
    krow = lax.broadcasted_iota(jnp.int32, (tk, n_rows), 0)
    acc_ref[...] = jnp.zeros_like(acc_ref)
    run_ref[...] = jnp.zeros_like(run_ref)
    last = (q0 + tq - 1) // tk

    def step(j, diagonal):
        start = pl.multiple_of(j * tk, tk)
        k = k_ref[pl.ds(start, tk), :].astype(BF16)
        vt = vt_ref[:, pl.ds(start, tk)].astype(BF16)
        z = _dot(k, q4t)
        log_keep = -_softplus(z)
        if diagonal:
            mask = (krow + j * tk) < qpos
            log_keep = jnp.where(mask, log_keep, 0.0)
        hi, lo = _split_bf16(log_keep)
        both = _dot(u_t, jnp.concatenate([hi, lo], axis=1))
        later = both[:, :n_rows] + both[:, n_rows:] + run_ref[...]
        a = jnp.exp(z + log_keep + later)
        if diagonal:
            a = jnp.where(mask, a, 0.0)
        acc_ref[...] += _dot(vt, a.astype(BF16))
        run_ref[...] += jnp.sum(log_keep, axis=0, keepdims=True)

    step(last, True)

    def body(jj, carry):
        step(last - 1 - jj, False)
        return carry

    lax.fori_loop(0, last, body, 0)
    acc = acc_ref[...]
    out_t = jnp.concatenate([acc[h * HEAD_DIM:(h + 1) * HEAD_DIM, h * tq:(h + 1) * tq] for h in range(SB_HEADS)],
                            axis=0)
    o_ref[...] = out_t.T * _silu(g_ref[...])


def _sb_prompt(sbqt, sbkv, sbvt, sbg, batch, seq):
    tq = Q_TILE
    nq = seq // tq
    return pl.pallas_call(
        _sb_prompt_kernel,
        grid=(batch, nq),
        in_specs=[pl.BlockSpec((256, tq), lambda b, i: (0, b * nq + i)),
                  pl.BlockSpec((seq, 256), lambda b, i: (b, 0)),
                  pl.BlockSpec((256, seq), lambda b, i: (0, b)),
                  pl.BlockSpec((tq, 256), lambda b, i: (b * nq + i, 0))],
        out_specs=pl.BlockSpec((tq, 256), lambda b, i: (b * nq + i, 0)),
        out_shape=jax.ShapeDtypeStruct((batch * seq, 256), F32),
        scratch_shapes=[pltpu.VMEM((SB_WIDTH, SB_HEADS * tq), F32), pltpu.VMEM((1, SB_HEADS * tq), F32)],
        compiler_params=pltpu.CompilerParams(dimension_semantics=("parallel", "parallel"),
                                             vmem_limit_bytes=VMEM_LIMIT),
        name="sb_prompt",
    )(sbqt, sbkv, sbvt, sbg)


def _stack_sb_heads(q):
    head = lax.broadcasted_iota(jnp.int32, q.shape, 1) // HEAD_DIM
    return jnp.concatenate([jnp.where(head == h, q, 0.0) for h in range(SB_HEADS)], axis=0).astype(BF16)


def _unstack_sb_heads(acc, t):
    head = lax.broadcasted_iota(jnp.int32, (t, SB_WIDTH), 1) // HEAD_DIM
    out = jnp.zeros((t, SB_WIDTH), F32)
    for h in range(SB_HEADS):
        out = jnp.where(head == h, acc[h * t:(h + 1) * t], out)
    return out


def _sb_sample_kernel(pt_ref, q_ref, kvn_ref, g_ref, *rest, n_pages, page):
    pages = rest[:n_pages]
    o_ref = rest[n_pages]
    new_ref = rest[n_pages + 1]
    t = q_ref.shape[0]
    rows = SB_HEADS * t
    q4 = _stack_sb_heads(q_ref[...] * SCALE)
    u = _later_matrix(page)
    trow = lax.broadcasted_iota(jnp.int32, (rows, page), 0) % t
    scol = lax.broadcasted_iota(jnp.int32, (rows, page), 1)

    new_ref[...] = jnp.zeros_like(new_ref)
    new_ref[0:t, :] = kvn_ref[...]
    mask = scol < trow
    z_new = _dot_nt(q4, new_ref[:, 0:256].astype(BF16))
    order = list(reversed(range(n_pages)))
    zs = [z_new] + [_dot(q4, pages[p][0, 0, 0:256, :].astype(BF16)) for p in order]
    log_keeps = [jnp.where(mask, -_softplus(z_new), 0.0)] + [-_softplus(z) for z in zs[1:]]
    halves = []
    for lk in log_keeps:
        halves += list(_split_bf16(lk))
    both = _dot(jnp.concatenate(halves, axis=0), u)
    run = jnp.zeros((rows, 1), F32)
    acc = jnp.zeros((rows, SB_WIDTH), F32)
    for idx, (z, lk) in enumerate(zip(zs, log_keeps)):
        later = both[2 * idx * rows:(2 * idx + 1) * rows] + both[(2 * idx + 1) * rows:(2 * idx + 2) * rows] + run
        a = jnp.exp(z + lk + later)
        if idx == 0:
            acc = acc + _dot(jnp.where(mask, a, 0.0).astype(BF16), new_ref[:, 256:512].astype(BF16))
        else:
            acc = acc + _dot_nt(a.astype(BF16), pages[order[idx - 1]][0, 0, 256:512, :].astype(BF16))
        run = run + jnp.sum(lk, axis=1, keepdims=True)
    o_ref[...] = _unstack_sb_heads(acc, t) * _silu(g_ref[...])


def _page_specs(layer, n_pages, page):
    def spec(p):
        return pl.BlockSpec((1, 1, 512, page), lambda b, pt: (layer, pt[b, p], 0, 0))
    return [spec(p) for p in range(n_pages)]


def _sb_sample(page_table, sbq, sbkv, sbg, cache_t, layer, row0, dec_batch, t):
    n_pages = page_table.shape[1]
    page = cache_t.shape[3]
    blk0 = row0 // t
    own = lambda w: pl.BlockSpec((t, w), lambda b, pt: (b, 0))
    joint = lambda w: pl.BlockSpec((t, w), lambda b, pt: (blk0 + b, 0))
    grid_spec = pltpu.PrefetchScalarGridSpec(
        num_scalar_prefetch=1,
        grid=(dec_batch,),
        in_specs=[own(256), joint(512), joint(256)] + _page_specs(layer, n_pages, page),
        out_specs=pl.BlockSpec((t, 256), lambda b, pt: (b, 0)),
        scratch_shapes=[pltpu.VMEM((page, 512), F32)],
    )
    return pl.pallas_call(
        functools.partial(_sb_sample_kernel, n_pages=n_pages, page=page),
        grid_spec=grid_spec,
        out_shape=jax.ShapeDtypeStruct((dec_batch * t, 256), F32),
        compiler_params=pltpu.CompilerParams(dimension_semantics=("parallel",), vmem_limit_bytes=VMEM_LIMIT),
        name="sb_sample",
    )(page_table, sbq, sbkv, sbg, *([cache_t] * n_pages))


def _pool_kernel(u_ref, g_ref, w_ref, sc_ref, *rest, nb, t, pos0, has_prefix):
    if has_prefix:
        pre_ref, o_ref, rows_ref = rest
    else:
        o_ref, rows_ref = rest
    pad = POOL_BUF + 1
    u = u_ref[...].reshape(nb, t, POOL_WIDTH)
    rows_ref[:, 0:pad, :] = jnp.zeros((nb, pad, POOL_WIDTH), F32)
    if has_prefix:
        rows_ref[:, 1:pad, :] = pre_ref[0]
    rows_ref[:, pad:pad + t, :] = u
    back = lambda s: rows_ref[:, pad - s:pad - s + t, :]
    sums = {}
    total = u
    done = 1
    for w in POOL_WINDOWS:
        for s in range(done, w):
            total = total + back(s)
        done = w
        sums[w] = total
    lane = lax.broadcasted_iota(jnp.int32, (nb, t, POOL_WIDTH), 2) // POOL_GDIM
    pos = pos0 + lax.broadcasted_iota(jnp.int32, (nb, t, POOL_WIDTH), 1)
    window = jnp.zeros((nb, t, POOL_WIDTH), jnp.int32)
    picked = jnp.zeros((nb, t, POOL_WIDTH), F32)
    for gi, w in enumerate(POOL_WINDOWS):
        window = jnp.where(lane == gi, w, window)
        picked = jnp.where(lane == gi, sums[w], picked)
    count = jnp.minimum(pos + 1, window).astype(F32)
    d = picked / count - u
    y = _dot(d.reshape(nb * t, POOL_WIDTH).astype(BF16), w_ref[...]) * sc_ref[...]
    o_ref[...] = y * _silu(g_ref[...])


def _pool(pu, pg, w_bd, scale, prefix, layer, row0, nb, t, n_seq, pos0):
    rows = nb * t
    blk0 = row0 // rows
    has_prefix = prefix is not None
    in_specs = [pl.BlockSpec((rows, 256), lambda i: (blk0 + i, 0)),
                pl.BlockSpec((rows, 256), lambda i: (blk0 + i, 0)),
                pl.BlockSpec((256, 256), lambda i: (0, 0)),
                pl.BlockSpec((1, 256), lambda i: (0, 0))]
    args = [pu, pg, w_bd, scale]
    if has_prefix:
        in_specs.append(pl.BlockSpec((1, nb, POOL_BUF, 256), lambda i: (layer, i, 0, 0)))
        args.append(prefix)
    return pl.pallas_call(
        functools.partial(_pool_kernel, nb=nb, t=t, pos0=pos0, has_prefix=has_prefix),
        grid=(n_seq // nb,),
        in_specs=in_specs,
        out_specs=pl.BlockSpec((rows, 256), lambda i: (i, 0)),
        out_shape=jax.ShapeDtypeStruct((n_seq * t, 256), F32),
        scratch_shapes=[pltpu.VMEM((nb, POOL_BUF + 1 + t, 256), F32)],
        compiler_params=pltpu.CompilerParams(dimension_semantics=("parallel",), vmem_limit_bytes=VMEM_LIMIT),
        name="pool_sample" if has_prefix else "pool_prompt",
    )(*args)


def _stack_nsa_heads_t(ref, t):
    low = lax.broadcasted_iota(jnp.int32, (LANES, t), 0) < HEAD_DIM
    g0, g1 = [], []
    for c in range(NSA_GROUP):
        ch = ref[c * LANES:(c + 1) * LANES, :] * SCALE
        g0.append(jnp.where(low, ch, 0.0))
        g1.append(jnp.where(low, 0.0, ch))
    return jnp.concatenate(g0 + g1, axis=1).astype(BF16)


def _per_group_lanes(x, t):
    a, b = x[:, :t], x[:, t:]
    return jnp.concatenate([a] * NSA_GROUP + [b] * NSA_GROUP, axis=1)


def _select_blocks_t(imp, qpos, n_sel):
    blk = lax.broadcasted_iota(jnp.int32, imp.shape, 0)
    forced = (blk == qpos // SEL_BLOCK) | (blk == 0)
    future = blk * SEL_BLOCK > qpos
    imp = jnp.where(forced, FORCED_SCORE, jnp.where(future, -1.0, imp))
    rank = jnp.zeros(imp.shape, jnp.int32)
    for i in range(n_sel):
        row = imp[i:i + 1, :]
        ahead = (row > imp) | ((row == imp) & (blk > i))
        rank = rank + ahead.astype(jnp.int32)
    return (rank < min(SEL_TOPN, n_sel)).astype(F32)


def _nsa_prompt_kernel(nqct_ref, nqrt_ref, bgt_ref, ng_ref, cmp_ref, ks_ref, vst_ref, wk_ref, wvt_ref,
                       o_ref, m_ref, l_ref, acc_ref):
    tq = nqct_ref.shape[1]
    tk = K_TILE
    n_sel = cmp_ref.shape[0]
    i = pl.program_id(1)
    q0 = i * tq
    n_rows = NSA_HEADS * tq
    qct = _stack_nsa_heads_t(nqct_ref, tq)
    qrt = _stack_nsa_heads_t(nqrt_ref, tq)
    qpos = q0 + lax.broadcasted_iota(jnp.int32, (1, n_rows), 1) % tq
    qpos2 = q0 + lax.broadcasted_iota(jnp.int32, (1, NSA_KV_HEADS * tq), 1) % tq

    s_e = _dot(cmp_ref[:, 0:128].astype(BF16), qct)
    s_o = _dot(cmp_ref[:, 128:256].astype(BF16), qct)
    blk = lax.broadcasted_iota(jnp.int32, (n_sel, n_rows), 0)
    s_e = jnp.where(blk * SEL_BLOCK + (CMP_BLOCK - 1) <= qpos, s_e, NEG_INF)
    s_o = jnp.where(blk * SEL_BLOCK + (SEL_BLOCK - 1) <= qpos, s_o, NEG_INF)
    m = jnp.maximum(jnp.max(s_e, axis=0, keepdims=True), jnp.max(s_o, axis=0, keepdims=True))
    m = jnp.where(m == NEG_INF, 0.0, m)
    p_e = jnp.exp(s_e - m)
    p_o = jnp.exp(s_o - m)
    denom = jnp.maximum(jnp.sum(p_e, axis=0, keepdims=True) + jnp.sum(p_o, axis=0, keepdims=True), 1e-30)
    p_e = p_e / denom
    p_o = p_o / denom
    o_c = (_dot_tn(cmp_ref[:, 256:384].astype(BF16), p_e.astype(BF16))
           + _dot_tn(cmp_ref[:, 384:512].astype(BF16), p_o.astype(BF16)))
    pair = p_e + p_o
    imp = []
    for g in range(NSA_KV_HEADS):
        acc = pair[:, (g * NSA_GROUP) * tq:(g * NSA_GROUP + 1) * tq]
        for c in range(1, NSA_GROUP):
            acc = acc + pair[:, (g * NSA_GROUP + c) * tq:(g * NSA_GROUP + c + 1) * tq]
        imp.append(acc)
    sel = _select_blocks_t(jnp.concatenate(imp, axis=1), qpos2, n_sel)

    def reset():
        m_ref[...] = jnp.full_like(m_ref, NEG_INF)
        l_ref[...] = jnp.zeros_like(l_ref)
        acc_ref[...] = jnp.zeros_like(acc_ref)

    def online_softmax(s, vt):
        m_old = m_ref[...]
        m_new = jnp.maximum(m_old, jnp.max(s, axis=0, keepdims=True))
        p = jnp.exp(s - m_new)
        alpha = jnp.exp(m_old - m_new)
        l_ref[...] = alpha * l_ref[...] + jnp.sum(p, axis=0, keepdims=True)
        acc_ref[...] = alpha * acc_ref[...] + _dot(vt, p.astype(BF16))
        m_ref[...] = m_new

    def finish():
        return acc_ref[...] / jnp.maximum(l_ref[...], 1e-30)

    krow = lax.broadcasted_iota(jnp.int32, (tk, n_rows), 0)
    last = (q0 + tq - 1) // tk

    member_bias = _per_group_lanes((sel - 1.0) * (-MASKED), tq).astype(BF16)
    q_aug = jnp.concatenate([qrt, member_bias], axis=0)
    reset()

    def slc_step(j, diagonal):
        start = pl.multiple_of(j * tk, tk)
        key_blk = (j * tk + lax.broadcasted_iota(jnp.int32, (tk, n_sel), 0)) // SEL_BLOCK
        onehot = (key_blk == lax.broadcasted_iota(jnp.int32, (tk, n_sel), 1)).astype(BF16)
        k_aug = jnp.concatenate([ks_ref[pl.ds(start, tk), :].astype(BF16), onehot], axis=1)
        s = _dot(k_aug, q_aug)
        if diagonal:
            s = jnp.where(krow + j * tk <= qpos, s, MASKED)
        online_softmax(s, vst_ref[:, pl.ds(start, tk)].astype(BF16))

    def slc_body(j, carry):
        slc_step(j, False)
        return carry

    lax.fori_loop(0, last, slc_body, 0)
    slc_step(last, True)
    o_s = finish()

    reset()

    def win_body(j, carry):
        start = pl.multiple_of(j * tk, tk)
        s = _dot(wk_ref[pl.ds(start, tk), :].astype(BF16), qrt)
        dist = qpos - (krow + j * tk)
        s = jnp.where((dist >= 0) & (dist < WINDOW), s, MASKED)
        online_softmax(s, wvt_ref[:, pl.ds(start, tk)].astype(BF16))
        return carry

    lax.fori_loop(jnp.maximum(q0 - WINDOW + 1, 0) // tk, last + 1, win_body, 0)
    o_w = finish()

    gate = 1.0 / (1.0 + jnp.exp(-bgt_ref[...]))
    low = lax.broadcasted_iota(jnp.int32, (LANES, tq), 0) < HEAD_DIM
    for c in range(NSA_GROUP):
        per_group = []
        for g in range(NSA_KV_HEADS):
            hd = g * NSA_GROUP + c
            sl = slice(hd * tq, (hd + 1) * tq)
            per_group.append(gate[3 * hd:3 * hd + 1, :] * o_c[:, sl] + gate[3 * hd + 1:3 * hd + 2, :] * o_s[:, sl]
                             + gate[3 * hd + 2:3 * hd + 3, :] * o_w[:, sl])
        mixed = jnp.where(low, per_group[0], per_group[1])
        cs = slice(c * LANES, (c + 1) * LANES)
        o_ref[:, cs] = mixed.T * _silu(ng_ref[:, cs])


def _nsa_prompt(nqct, nqrt, bgt, ngate, cmp, nsa_new, nsvt, win_new, nwvt, batch, seq):
    tq = Q_TILE
    nq = seq // tq
    n_sel = seq // SEL_BLOCK
    qcol = lambda f: pl.BlockSpec((f, tq), lambda b, i: (0, b * nq + i))
    qrow = lambda w: pl.BlockSpec((tq, w), lambda b, i: (b * nq + i, 0))
    n_rows = NSA_HEADS * tq
    return pl.pallas_call(
        _nsa_prompt_kernel,
        grid=(batch, nq),
        in_specs=[qcol(512), qcol(512), qcol(128), qrow(512),
                  pl.BlockSpec((n_sel, 512), lambda b, i: (b, 0)),
                  pl.BlockSpec((seq, LANES), lambda b, i: (b, 2)),
                  pl.BlockSpec((LANES, seq), lambda b, i: (0, b)),
                  pl.BlockSpec((seq, LANES), lambda b, i: (b, 0)),
                  pl.BlockSpec((LANES, seq), lambda b, i: (0, b))],
        out_specs=qrow(512),
        out_shape=jax.ShapeDtypeStruct((batch * seq, 512), F32),
        scratch_shapes=[pltpu.VMEM((1, n_rows), F32), pltpu.VMEM((1, n_rows), F32), pltpu.VMEM((LANES, n_rows), F32)],
        compiler_params=pltpu.CompilerParams(dimension_semantics=("parallel", "parallel"),
                                             vmem_limit_bytes=VMEM_LIMIT),
        name="nsa_prompt",
    )(nqct, nqrt, bgt, ngate, cmp, nsa_new, nsvt, win_new, nwvt)


def _stack_nsa_heads(ref, t):
    low = lax.broadcasted_iota(jnp.int32, (t, LANES), 1) < HEAD_DIM
    g0, g1 = [], []
    for c in range(NSA_GROUP):
        ch = ref[:, c * LANES:(c + 1) * LANES] * SCALE
        g0.append(jnp.where(low, ch, 0.0))
        g1.append(jnp.where(low, 0.0, ch))
    return jnp.concatenate(g0 + g1, axis=0).astype(BF16)


def _per_group_rows(x, t):
    a, b = x[:t], x[t:]
    return jnp.concatenate([a] * NSA_GROUP + [b] * NSA_GROUP, axis=0)


def _select_blocks(imp, qpos, n_sel):
    blk = lax.broadcasted_iota(jnp.int32, imp.shape, 1)
    forced = (blk == qpos // SEL_BLOCK) | (blk == 0)
    future = blk * SEL_BLOCK > qpos
    imp = jnp.where(forced, FORCED_SCORE, jnp.where(future, -1.0, imp))
    rank = jnp.zeros(imp.shape, jnp.int32)
    for i in range(n_sel):
        col = imp[:, i:i + 1]
        ahead = (col > imp) | ((col == imp) & (blk > i))
        rank = rank + ahead.astype(jnp.int32)
    return ((rank < min(SEL_TOPN, n_sel)) & (blk < n_sel)).astype(BF16)


def _expand_matrix(n_rows, n_keys, key0):
    r = lax.broadcasted_iota(jnp.int32, (n_rows, n_keys), 0)
    c = lax.broadcasted_iota(jnp.int32, (n_rows, n_keys), 1)
    return ((key0 + c) // SEL_BLOCK == r).astype(BF16)


def _softmax_parts(parts):
    m = None
    for s, _, _ in parts:
        pm = jnp.max(s, axis=1, keepdims=True)
        m = pm if m is None else jnp.maximum(m, pm)
    m = jnp.where(m == NEG_INF, 0.0, m)
    denom = None
    acc = None
    for s, v, feature_major in parts:
        p = jnp.exp(s - m)
        ps = jnp.sum(p, axis=1, keepdims=True)
        pv = _dot_nt(p.astype(BF16), v) if feature_major else _dot(p.astype(BF16), v)
        denom = ps if denom is None else denom + ps
        acc = pv if acc is None else acc + pv
    return acc / jnp.maximum(denom, 1e-30)


def _nsa_sample_kernel(pt_ref, nqc_ref, nqr_ref, new_ref, winn_ref, bg_ref, ng_ref, swin_ref, phi_ref, seg_ref,
                       *rest, n_pages, page, pos0):
    pages = rest[:n_pages]
    o_ref, ckall_ref, cvall_ref, newpad_ref, winpad_ref = rest[n_pages:]
    t = nqc_ref.shape[0]
    rows = NSA_HEADS * t
    n_sel = (pos0 + t + SEL_BLOCK - 1) // SEL_BLOCK
    qc = _stack_nsa_heads(nqc_ref, t)
    qr = _stack_nsa_heads(nqr_ref, t)
    trow = lax.broadcasted_iota(jnp.int32, (rows, 1), 0) % t
    trow2 = lax.broadcasted_iota(jnp.int32, (NSA_KV_HEADS * t, 1), 0) % t
    qpos, qpos2 = pos0 + trow, pos0 + trow2

    for p in range(n_pages):
        ckall_ref[:, p * page:(p + 1) * page] = pages[p][0, 0, 0:128, :] * phi_ref[0:1, :]
        cvall_ref[:, p * page:(p + 1) * page] = pages[p][0, 0, 128:256, :] * phi_ref[1:2, :]
    seg = seg_ref[...]
    cmp_t = []
    for src_ref in (ckall_ref, cvall_ref):
        hi, lo = _split_bf16(src_ref[...])
        both = _dot(jnp.concatenate([hi, lo], axis=0), seg)
        cmp_t.append((both[:LANES] + both[LANES:]).astype(BF16))
    kcmp_t, vcmp_t = cmp_t
    s = _dot(qc, kcmp_t)
    col = lax.broadcasted_iota(jnp.int32, (rows, LANES), 1)
    block_end = jnp.where(col < SEL_BLOCK, col * SEL_BLOCK + (CMP_BLOCK - 1),
                          (col - SEL_BLOCK) * SEL_BLOCK + (SEL_BLOCK - 1))
    s = jnp.where(block_end <= qpos, s, NEG_INF)
    m = jnp.max(s, axis=1, keepdims=True)
    m = jnp.where(m == NEG_INF, 0.0, m)
    p_c = jnp.exp(s - m)
    p_c = p_c / jnp.maximum(jnp.sum(p_c, axis=1, keepdims=True), 1e-30)
    o_c = _dot_nt(p_c.astype(BF16), vcmp_t)
    imp = []
    for g in range(NSA_KV_HEADS):
        acc = p_c[(g * NSA_GROUP) * t:(g * NSA_GROUP + 1) * t]
        for c in range(1, NSA_GROUP):
            acc = acc + p_c[(g * NSA_GROUP + c) * t:(g * NSA_GROUP + c + 1) * t]
        imp.append(acc)
    imp = jnp.concatenate(imp, axis=0)
    imp = imp + pltpu.roll(imp, SEL_BLOCK, 1)
    sel = _select_blocks(imp, qpos2, n_sel)

    newpad_ref[...] = jnp.zeros_like(newpad_ref)
    newpad_ref[0:t, :] = new_ref[...]
    winpad_ref[...] = jnp.zeros_like(winpad_ref)
    winpad_ref[0:t, :] = winn_ref[...]
    scol = lax.broadcasted_iota(jnp.int32, (NSA_KV_HEADS * t, page), 1)

    member = _dot(sel, _expand_matrix(LANES, n_pages * page, 0))
    bias_past = _per_group_rows(jnp.where(member > 0.5, 0.0, NEG_INF), t)
    parts = []
    for p in range(n_pages):
        sc = _dot(qr, pages[p][0, 0, 256:384, :].astype(BF16)) + bias_past[:, p * page:(p + 1) * page]
        parts.append((sc, pages[p][0, 0, 384:512, :].astype(BF16), True))
    member_new = _dot(sel, _expand_matrix(LANES, page, n_pages * page))
    ok_new = (member_new > 0.5) & (scol <= trow2) & (scol < t)
    bias_new = _per_group_rows(jnp.where(ok_new, 0.0, NEG_INF), t)
    parts.append((_dot_nt(qr, newpad_ref[:, 256:384].astype(BF16)) + bias_new,
                  newpad_ref[:, 384:512].astype(BF16), False))
    o_s = _softmax_parts(parts)

    wlen = swin_ref.shape[3]
    rcol = lax.broadcasted_iota(jnp.int32, (NSA_KV_HEADS * t, wlen), 1)
    dist = wlen + trow2 - rcol
    bias_w = _per_group_rows(jnp.where((dist >= 0) & (dist < WINDOW), 0.0, NEG_INF), t)
    dist_n = trow2 - scol
    bias_wn = _per_group_rows(jnp.where((dist_n >= 0) & (dist_n < WINDOW) & (scol < t), 0.0, NEG_INF), t)
    parts = [(_dot(qr, swin_ref[0, 0, 0:128, :].astype(BF16)) + bias_w, swin_ref[0, 0, 128:256, :].astype(BF16), True),
             (_dot_nt(qr, winpad_ref[:, 0:128].astype(BF16)) + bias_wn, winpad_ref[:, 128:256].astype(BF16), False)]
    o_w = _softmax_parts(parts)

    gate = 1.0 / (1.0 + jnp.exp(-bg_ref[...]))
    low = lax.broadcasted_iota(jnp.int32, (t, LANES), 1) < HEAD_DIM
    for c in range(NSA_GROUP):
        per_group = []
        for g in range(NSA_KV_HEADS):
            r0 = (g * NSA_GROUP + c) * t
            cg = (g * NSA_GROUP + c) * 3
            per_group.append(gate[:, cg:cg + 1] * o_c[r0:r0 + t] + gate[:, cg + 1:cg + 2] * o_s[r0:r0 + t]
                             + gate[:, cg + 2:cg + 3] * o_w[r0:r0 + t])
        mixed = jnp.where(low, per_group[0], per_group[1])
        sl = slice(c * LANES, (c + 1) * LANES)
        o_ref[:, sl] = mixed * _silu(ng_ref[:, sl])


def _nsa_sample(page_table, nqc, nqr, nsa_new, win_new, bgate, ngate, state_win_t, phi_rows, seg, cache_t,
                layer, row0, dec_batch, t, pos0):
    n_pages = page_table.shape[1]
    page = cache_t.shape[3]
    blk0 = row0 // t
    own = lambda w: pl.BlockSpec((t, w), lambda b, pt: (b, 0))
    joint = lambda w: pl.BlockSpec((t, w), lambda b, pt: (blk0 + b, 0))
    wlen = state_win_t.shape[3]
    n_keys = n_pages * page
    grid_spec = pltpu.PrefetchScalarGridSpec(
        num_scalar_prefetch=1,
        grid=(dec_batch,),
        in_specs=[own(512), own(512), joint(512), joint(256), own(128), joint(512),
                  pl.BlockSpec((1, 1, 256, wlen), lambda b, pt: (layer, b, 0, 0)),
                  pl.BlockSpec((2, page), lambda b, pt: (0, 0)),
                  pl.BlockSpec((n_keys, LANES), lambda b, pt: (0, 0))]
        + _page_specs(layer, n_pages, page),
        out_specs=pl.BlockSpec((t, 512), lambda b, pt: (b, 0)),
        scratch_shapes=[pltpu.VMEM((LANES, n_keys), F32), pltpu.VMEM((LANES, n_keys), F32),
                        pltpu.VMEM((page, 512), F32), pltpu.VMEM((page, 256), F32)],
    )
    return pl.pallas_call(
        functools.partial(_nsa_sample_kernel, n_pages=n_pages, page=page, pos0=pos0),
        grid_spec=grid_spec,
        out_shape=jax.ShapeDtypeStruct((dec_batch * t, 512), F32),
        compiler_params=pltpu.CompilerParams(dimension_semantics=("parallel",), vmem_limit_bytes=VMEM_LIMIT),
        name="nsa_sample",
    )(page_table, nqc, nqr, nsa_new, win_new, bgate, ngate, state_win_t, phi_rows, seg, *([cache_t] * n_pages))


def _out_proj_kernel(oa_ref, ob_ref, oc_ref, x_ref, w_ref, g_ref, y_ref):
    out = (_dot(oa_ref[...].astype(BF16), w_ref[0:256, :])
           + _dot(ob_ref[...].astype(BF16), w_ref[256:512, :])
           + _dot(oc_ref[...].astype(BF16), w_ref[512:1024, :]))
    ms = jnp.mean(out * out, axis=-1, keepdims=True)
    y_ref[...] = x_ref[...] + out * lax.rsqrt(ms + NORM_EPS) * g_ref[...]


def _out_proj(oa, ob, oc, x, w_out_p, g_post):
    n = x.shape[0]
    tm = ROW_TILE
    row = lambda w: pl.BlockSpec((tm, w), lambda i: (i, 0))
    fixed = lambda s: pl.BlockSpec(s, lambda i: (0, 0))
    return pl.pallas_call(
        _out_proj_kernel,
        grid=(n // tm,),
        in_specs=[row(256), row(256), row(512), row(D_MODEL), fixed((D_MODEL, D_MODEL)), fixed((1, D_MODEL))],
        out_specs=row(D_MODEL),
        out_shape=jax.ShapeDtypeStruct((n, D_MODEL), F32),
        compiler_params=pltpu.CompilerParams(dimension_semantics=("parallel",), vmem_limit_bytes=VMEM_LIMIT),
        name="out_proj",
    )(oa, ob, oc, x, w_out_p, g_post)


def _head_permutation():
    perm = []
    for c in range(NSA_GROUP):
        for g in range(NSA_KV_HEADS):
            h = g * NSA_GROUP + c
            perm.extend(range(h * HEAD_DIM, (h + 1) * HEAD_DIM))
    return np.asarray(perm, np.int32)


def _rope_tables(pos):
    inv = ROPE_THETA ** (-jnp.arange(ROT_HALF, dtype=F32) * (2.0 / ROT_DIM))
    ang = pos.astype(F32)[:, None] * inv[None, :]
    cos, sin = jnp.cos(ang), jnp.sin(ang)
    n = pos.shape[0]
    cos_h = jnp.concatenate([cos, cos, jnp.ones((n, HEAD_DIM - ROT_DIM), F32)], axis=1)
    sin_h = jnp.concatenate([-sin, sin, jnp.zeros((n, HEAD_DIM - ROT_DIM), F32)], axis=1)
    reps = (1, LANES // HEAD_DIM)
    return jnp.tile(cos_h, reps), jnp.tile(sin_h, reps), cos.T, sin.T


def _segment_matrix(n_keys):
    n = np.arange(n_keys) // CMP_BLOCK
    col = (n % 2) * SEL_BLOCK + n // 2
    return jnp.asarray(col[:, None] == np.arange(LANES)[None, :], BF16)


def kernel(x_prompt, x_sample, cache_sb_kv, cache_nsa_kv, state_win_kv, state_pool, page_table, norm_pre, w_in,
           pool_w, pool_scale, phi_k, phi_v, w_out, norm_post):
    bp, tp, d = x_prompt.shape
    bs, ts, _ = x_sample.shape
    depth = w_in.shape[0]
    n_pool, page = cache_sb_kv.shape[1], cache_sb_kv.shape[2]
    past_len = page_table.shape[1] * page
    wlen = state_win_kv.shape[2]
    n_p, n_s = bp * tp, bs * ts
    assert d == D_MODEL and n_p % ROW_TILE == 0 and n_s % ROW_TILE == 0
    assert tp % K_TILE == 0 and K_TILE % Q_TILE == 0 and tp % SEL_BLOCK == 0 and n_p % n_s == 0
    assert past_len % SEL_BLOCK == 0 and ts < CMP_BLOCK and ts % 8 == 0 and page == LANES
    assert past_len // CMP_BLOCK <= SEL_BLOCK and (wlen >= WINDOW or wlen == past_len)

    perm = _head_permutation()
    o = np.cumsum((0,) + (256,) * 6 + (512,) + (128,) * 6 + (24, 512))
    cols = np.concatenate([np.arange(o[0], o[6]), o[6] + perm, np.arange(o[7], o[13]), o[14] + perm,
                           np.arange(o[13], o[14])])
    w_in_p = jnp.pad(w_in[:, :, cols], ((0, 0), (0, 0), (0, IN_WIDTH_P - cols.size))).astype(BF16)
    feats = np.concatenate([np.arange(C_SBQ, C_SBQ + 256), np.arange(C_SBV, C_SBV + 256), np.arange(C_NQ, C_NQ + 512),
                            np.arange(C_NSA + 384, C_NSA + 512), np.arange(C_WIN + 128, C_WIN + 256),
                            np.arange(C_BG, C_BG + 128)])
    w_in_t = jnp.swapaxes(w_in_p[:, :, feats], 1, 2)
    rows = np.concatenate([np.arange(512), 512 + perm])
    w_out_p = w_out[:, rows, :].astype(BF16)
    w_pool = jnp.zeros((depth, POOL_WIDTH, POOL_WIDTH), F32)
    for g in range(len(POOL_WINDOWS)):
        sl = slice(g * POOL_GDIM, (g + 1) * POOL_GDIM)
        w_pool = w_pool.at[:, sl, sl].set(pool_w[:, g])
    w_pool = w_pool.astype(BF16)
    reps = page // CMP_BLOCK
    phi_rows = jnp.stack([jnp.tile(phi_k, (1, reps)), jnp.tile(phi_v, (1, reps))], axis=1)
    seg = _segment_matrix(past_len)

    pos = jnp.concatenate([jnp.tile(jnp.arange(tp, dtype=jnp.int32), bp),
                           jnp.tile(past_len + jnp.arange(ts, dtype=jnp.int32), bs)])
    tables = _rope_tables(pos)

    cache_sb_t = jnp.transpose(cache_sb_kv, (0, 1, 3, 4, 5, 2)).reshape(depth, n_pool, 512, page)
    cache_nsa_t = jnp.transpose(cache_nsa_kv, (0, 1, 3, 4, 5, 2)).reshape(depth, n_pool, 512, page)
    state_win_t = jnp.transpose(state_win_kv, (0, 1, 3, 4, 5, 2)).reshape(depth, bs, 256, wlen)

    x = jnp.concatenate([x_prompt.reshape(n_p, d), x_sample.reshape(n_s, d)], axis=0)
    outs = [[] for _ in range(8)]
    for layer in range(depth):
        (sbkv, sbg, pu, pg, nsa_new, win_new, ngate, cmp, sbq, nqc, nqr, bgate,
         sbqt, sbvt, nqct, nqrt, nsvt, nwvt, bgt) = _in_proj(
            x, norm_pre[layer][None], w_in_p[layer], w_in_t[layer], phi_k[layer], phi_v[layer], tables, n_p)
        oa = jnp.concatenate([
            _sb_prompt(sbqt, sbkv, sbvt, sbg, bp, tp),
            _sb_sample(page_table, sbq, sbkv, sbg, cache_sb_t, layer, n_p, bs, ts)], axis=0)
        ob = jnp.concatenate([
            _pool(pu, pg, w_pool[layer], pool_scale[layer][None], None, layer, 0, 1, tp, bp, 0),
            _pool(pu, pg, w_pool[layer], pool_scale[layer][None], state_pool, layer, n_p, bs, ts, bs, past_len)],
            axis=0)
        oc = jnp.concatenate([
            _nsa_prompt(nqct, nqrt, bgt, ngate, cmp, nsa_new, nsvt, win_new, nwvt, bp, tp),
            _nsa_sample(page_table, nqc, nqr, nsa_new, win_new, bgate, ngate, state_win_t, phi_rows[layer], seg,
                        cache_nsa_t, layer, n_p, bs, ts, past_len)], axis=0)
        x = _out_proj(oa, ob, oc, x, w_out_p[layer], norm_post[layer][None])

        keep_p = min(WINDOW, tp)
        win_s = jnp.concatenate([state_win_kv[layer].reshape(bs, wlen, 256), win_new[n_p:].reshape(bs, ts, 256)],
                                axis=1)[:, -wlen:]
        pool_s = jnp.concatenate([state_pool[layer], pu[n_p:].reshape(bs, ts, 256)], axis=1)[:, -POOL_BUF:]
        outs[0].append(sbkv[:n_p].reshape(bp, tp, 2, SB_HEADS, HEAD_DIM))
        outs[1].append(nsa_new[:n_p].reshape(bp, tp, 4, NSA_KV_HEADS, HEAD_DIM))
        outs[2].append(win_new[:n_p].reshape(bp, tp, 2, NSA_KV_HEADS, HEAD_DIM)[:, tp - keep_p:])
        outs[3].append(pu[:n_p].reshape(bp, tp, POOL_WIDTH)[:, tp - POOL_BUF:])
        outs[4].append(sbkv[n_p:].reshape(bs, ts, 2, SB_HEADS, HEAD_DIM))
        outs[5].append(nsa_new[n_p:].reshape(bs, ts, 4, NSA_KV_HEADS, HEAD_DIM))
        outs[6].append(win_s.reshape(bs, wlen, 2, NSA_KV_HEADS, HEAD_DIM))
        outs[7].append(pool_s)
    y_prompt = x[:n_p].reshape(bp, tp, d)
    y_sample = x[n_p:].reshape(bs, ts, d)
    return (y_prompt, y_sample) + tuple(jnp.stack(o_) for o_ in outs)
```

```python
import functools

import jax
import jax.numpy as jnp
import numpy as np
from jax import lax
from jax.experimental import pallas as pl
from jax.experimental.pallas import tpu as pltpu

F32 = jnp.float32
BF16 = jnp.bfloat16

D_MODEL = 1024
HEAD_DIM = 64
SB_WIDTH = 256
SB_HEADS = 4
POOL_WINDOWS = (2, 4, 8, 16)
POOL_WIDTH = 256
POOL_GDIM = 64
POOL_BUF = 15
NSA_WIDTH = 512
NSA_HEADS = 8
NSA_KV_HEADS = 2
NSA_GROUP = 4
NSA_KV_WIDTH = 128
CMP_BLOCK = 32
SEL_BLOCK = 64
SEL_TOPN = 8
WINDOW = 512
ROPE_THETA = 500000.0
ROT_DIM = 16
ROT_HALF = ROT_DIM // 2
NORM_EPS = 1e-6
FORCED_SCORE = 1e4
SCALE = HEAD_DIM ** -0.5
NEG_INF = float("-inf")
MASKED = -(2.0 ** 100)

C_SBQ, C_SBK, C_SBV, C_SBG = 0, 256, 512, 768
C_PU, C_PG = 1024, 1280
C_NQ = 1536
C_NSA = 2048
C_WIN = 2560
C_NG = 2816
C_BG = 3328
IN_WIDTH_P = 3456
LANES = 128
ROW_TILE = 512
Q_TILE = 128
K_TILE = 256
VMEM_LIMIT = 56 * 1024 * 1024


def _dot(a, b):
    return jnp.dot(a, b, preferred_element_type=F32)


def _dot_nt(a, b):
    return lax.dot_general(a, b, (((1,), (1,)), ((), ())), preferred_element_type=F32)


def _dot_tn(a, b):
    return lax.dot_general(a, b, (((0,), (0,)), ((), ())), preferred_element_type=F32)


def _silu(g):
    return g / (1.0 + jnp.exp(-g))


def _softplus(z):
    return jnp.maximum(z, 0.0) + jnp.log(1.0 + jnp.exp(-jnp.abs(z)))


def _split_bf16(x):
    hi = x.astype(BF16)
    return hi, (x - hi.astype(F32)).astype(BF16)


def _suffix_sum_exclusive(x, u_bf16):
    hi, lo = _split_bf16(x)
    n = x.shape[0]
    both = _dot(jnp.concatenate([hi, lo], axis=0), u_bf16)
    return both[:n] + both[n:]


def _later_matrix(n, transposed=False):
    r = lax.broadcasted_iota(jnp.int32, (n, n), 0)
    c = lax.broadcasted_iota(jnp.int32, (n, n), 1)
    return ((c > r) if transposed else (r > c)).astype(BF16)


def _in_proj_kernel(phik_ref, phiv_ref, x_ref, g_ref, wt_ref, cos_ref, sin_ref, cost_ref, sint_ref,
                    sbkv_prev_ref, nsa_prev_ref,
                    sbkv_ref, sbg_ref, pu_ref, pg_ref, nsa_ref, win_ref, ng_ref, cmp_ref,
                    sbq_ref, nqc_ref, nqr_ref, bg_ref,
                    sbqt_ref, nqct_ref, nqrt_ref, bgt_ref, sbkvt_ref, nsat_ref, wint_ref,
                    ck_ref, cv_ref, *, n_prompt_tiles):
    del sbkv_prev_ref, nsa_prev_ref
    i = pl.program_id(0)
    x = x_ref[...]
    ms = jnp.mean(x * x, axis=-1, keepdims=True)
    h = (x * lax.rsqrt(ms + NORM_EPS) * g_ref[...]).astype(BF16)

    def proj(a, b):
        return _dot_nt(h, wt_ref[a:b, :])

    cos = cos_ref[...]
    sin = sin_ref[...]
    first = (lax.broadcasted_iota(jnp.int32, cos.shape, 1) % HEAD_DIM) < ROT_HALF

    def rope(v):
        swapped = jnp.where(first, pltpu.roll(v, LANES - ROT_HALF, 1), pltpu.roll(v, ROT_HALF, 1))
        return v * cos + swapped * sin

    sbkv_ref[...] = proj(C_SBK, C_SBK + 512)
    sbg_ref[...] = proj(C_SBG, C_SBG + 256)
    pu_ref[...] = proj(C_PU, C_PU + 256)
    pg_ref[...] = proj(C_PG, C_PG + 256)
    ng_ref[...] = proj(C_NG, C_NG + 512)
    ck_ref[...] = proj(C_NSA, C_NSA + 128)
    cv_ref[...] = proj(C_NSA + 128, C_NSA + 256)
    nsa_ref[:, 0:128] = ck_ref[...]
    nsa_ref[:, 128:256] = cv_ref[...]
    nsa_ref[:, 256:384] = rope(proj(C_NSA + 256, C_NSA + 384))
    nsa_ref[:, 384:512] = proj(C_NSA + 384, C_NSA + 512)
    win_ref[:, 0:128] = rope(proj(C_WIN, C_WIN + 128))
    win_ref[:, 128:256] = proj(C_WIN + 128, C_WIN + 256)

    nb = cmp_ref.shape[0]
    for part, (src_ref, phi_ref) in enumerate(((ck_ref, phik_ref), (cv_ref, phiv_ref))):
        acc_e = jnp.zeros((nb, LANES), F32)
        acc_o = jnp.zeros((nb, LANES), F32)
        for l in range(CMP_BLOCK):
            acc_e = acc_e + src_ref[pl.ds(l, nb, stride=SEL_BLOCK), :] * phi_ref[l]
            acc_o = acc_o + src_ref[pl.ds(CMP_BLOCK + l, nb, stride=SEL_BLOCK), :] * phi_ref[l]
        cmp_ref[:, part * 256:part * 256 + 128] = acc_e
        cmp_ref[:, part * 256 + 128:part * 256 + 256] = acc_o

    @pl.when(i >= n_prompt_tiles)
    def _():
        sbq_ref[...] = proj(C_SBQ, C_SBQ + 256)
        bg_ref[...] = proj(C_BG, C_BG + 128)
        for c in range(NSA_GROUP):
            q = proj(C_NQ + c * LANES, C_NQ + (c + 1) * LANES)
            nqc_ref[:, c * LANES:(c + 1) * LANES] = q
            nqr_ref[:, c * LANES:(c + 1) * LANES] = rope(q)

    @pl.when(i < n_prompt_tiles)
    def _():
        def proj_t(a, b):
            return _dot_nt(wt_ref[a:b, :], h)

        cos_t = cost_ref[...]
        sin_t = sint_ref[...]

        def rope_t(v):
            pieces = []
            for base in range(0, LANES, HEAD_DIM):
                x1 = v[base:base + ROT_HALF]
                x2 = v[base + ROT_HALF:base + ROT_DIM]
                pieces += [x1 * cos_t - x2 * sin_t, x2 * cos_t + x1 * sin_t, v[base + ROT_DIM:base + HEAD_DIM]]
            return jnp.concatenate(pieces, axis=0)

        sbqt_ref[...] = proj_t(C_SBQ, C_SBQ + 256)
        bgt_ref[...] = proj_t(C_BG, C_BG + 128)
        for c in range(NSA_GROUP):
            q = proj_t(C_NQ + c * LANES, C_NQ + (c + 1) * LANES)
            nqct_ref[c * LANES:(c + 1) * LANES, :] = q
            nqrt_ref[c * LANES:(c + 1) * LANES, :] = rope_t(q)
        sbkvt_ref[0, 0] = sbkv_ref[...].T
        nsat_ref[0, 0] = nsa_ref[...].T
        wint_ref[0] = win_ref[...].T


def _in_proj(x, g_pre, w_t, phi_k, phi_v, tables, sbkv_all, nsa_all, layer, n_p, batch, seq):
    n = x.shape[0]
    n_s = n - n_p
    tm = ROW_TILE
    npt = n_p // tm
    per_seq = seq // tm
    cos_r, sin_r, cos_t, sin_t = tables
    row = lambda w: pl.BlockSpec((tm, w), lambda i: (i, 0))
    fixed = lambda s: pl.BlockSpec(s, lambda i: (0, 0), pipeline_mode=pl.Buffered(1))
    smem = pl.BlockSpec(memory_space=pltpu.SMEM)
    anywhere = pl.BlockSpec(memory_space=pl.ANY)
    prompt_tile = lambda i: jnp.minimum(i, npt - 1)
    sample_row = lambda w: pl.BlockSpec((tm, w), lambda i: (jnp.maximum(i - npt, 0), 0))
    prompt_col = lambda f: pl.BlockSpec((f, tm), lambda i: (0, prompt_tile(i)))
    stacked = pl.BlockSpec((1, 1, 512, tm), lambda i: (layer, prompt_tile(i) // per_seq, 0, prompt_tile(i) % per_seq))
    per_batch = pl.BlockSpec((1, 256, tm), lambda i: (prompt_tile(i) // per_seq, 0, prompt_tile(i) % per_seq))
    all_widths = (512, 256, 256, 256, 512, 256, 512)
    sample_widths = (256, 512, 512, 128)
    prompt_feats = (256, 512, 512, 128)
    out_shape = ([jax.ShapeDtypeStruct((n, w), F32) for w in all_widths]
                 + [jax.ShapeDtypeStruct((n // SEL_BLOCK, 512), F32)]
                 + [jax.ShapeDtypeStruct((n_s, w), F32) for w in sample_widths]
                 + [jax.ShapeDtypeStruct((f, n_p), F32) for f in prompt_feats]
                 + [jax.ShapeDtypeStruct(sbkv_all.shape, F32), jax.ShapeDtypeStruct(nsa_all.shape, F32),
                    jax.ShapeDtypeStruct((batch, 256, seq), F32)])
    out_specs = ([row(w) for w in all_widths] + [pl.BlockSpec((tm // SEL_BLOCK, 512), lambda i: (i, 0))]
                 + [sample_row(w) for w in sample_widths] + [prompt_col(f) for f in prompt_feats]
                 + [stacked, stacked, per_batch])
    n_fixed_outs = len(all_widths) + 1 + len(sample_widths) + len(prompt_feats)
    return pl.pallas_call(
        functools.partial(_in_proj_kernel, n_prompt_tiles=npt),
        grid=(n // tm,),
        in_specs=[smem, smem, row(D_MODEL), fixed((1, D_MODEL)), fixed((IN_WIDTH_P, D_MODEL)), row(LANES), row(LANES),
                  pl.BlockSpec((ROT_HALF, tm), lambda i: (0, i)), pl.BlockSpec((ROT_HALF, tm), lambda i: (0, i)),
                  anywhere, anywhere],
        out_specs=out_specs,
        out_shape=out_shape,
        input_output_aliases={9: n_fixed_outs, 10: n_fixed_outs + 1},
        scratch_shapes=[pltpu.VMEM((tm, LANES), F32), pltpu.VMEM((tm, LANES), F32)],
        compiler_params=pltpu.CompilerParams(dimension_semantics=("arbitrary",), vmem_limit_bytes=VMEM_LIMIT),
        name="in_proj",
    )(phi_k, phi_v, x, g_pre, w_t, cos_r, sin_r, cos_t, sin_t, sbkv_all, nsa_all)


def _sb_prompt_kernel(qt_ref, k_ref, vt_ref, g_ref, o_ref, acc_ref, run_ref):
    tq = qt_ref.shape[1]
    tk = K_TILE
    i = pl.program_id(1)
    q0 = i * tq
    n_rows = SB_HEADS * tq
    qt = qt_ref[...] * SCALE
    head = lax.broadcasted_iota(jnp.int32, qt.shape, 0) // HEAD_DIM
    q4t = jnp.concatenate([jnp.where(head == h, qt, 0.0) for h in range(SB_HEADS)], axis=1).astype(BF16)
    u_t = _later_matrix(tk, transposed=True)
    qpos = q0 + lax.broadcasted_iota(jnp.int32, (tk, n_rows), 1) % tq
    krow = lax.broadcasted_iota(jnp.int32, (tk, n_rows), 0)
    acc_ref[...] = jnp.zeros_like(acc_ref)
    run_ref[...] = jnp.zeros_like(run_ref)
    last = (q0 + tq - 1) // tk

    def step(j, diagonal):
        start = pl.multiple_of(j * tk, tk)
        k = k_ref[pl.ds(start, tk), :].astype(BF16)
        vt = vt_ref[0, 0, :, pl.ds(start, tk)].astype(BF16)
        z = _dot(k, q4t)
        log_keep = -_softplus(z)
        if diagonal:
            mask = (krow + j * tk) < qpos
            log_keep = jnp.where(mask, log_keep, 0.0)
        hi, lo = _split_bf16(log_keep)
        both = _dot(u_t, jnp.concatenate([hi, lo], axis=1))
        later = both[:, :n_rows] + both[:, n_rows:] + run_ref[...]
        a = jnp.exp(z + log_keep + later)
        if diagonal:
            a = jnp.where(mask, a, 0.0)
        acc_ref[...] += _dot(vt, a.astype(BF16))
        run_ref[...] += jnp.sum(log_keep, axis=0, keepdims=True)

    step(last, True)

    def body(jj, carry):
        step(last - 1 - jj, False)
        return carry

    lax.fori_loop(0, last, body, 0)
    acc = acc_ref[...]
    out_t = jnp.concatenate([acc[h * HEAD_DIM:(h + 1) * HEAD_DIM, h * tq:(h + 1) * tq] for h in range(SB_HEADS)],
                            axis=0)
    o_ref[...] = out_t.T * _silu(g_ref[...])


def _sb_prompt(sbqt, sbkv, sbkv_all, sbg, layer, batch, seq):
    tq = Q_TILE
    nq = seq // tq
    return pl.pallas_call(
        _sb_prompt_kernel,
        grid=(batch, nq),
        in_specs=[pl.BlockSpec((256, tq), lambda b, i: (0, b * nq + i)),
                  pl.BlockSpec((seq, 256), lambda b, i: (b, 0)),
                  pl.BlockSpec((1, 1, 256, seq), lambda b, i: (layer, b, 1, 0)),
                  pl.BlockSpec((tq, 256), lambda b, i: (b * nq + i, 0))],
        out_specs=pl.BlockSpec((tq, 256), lambda b, i: (b * nq + i, 0)),
        out_shape=jax.ShapeDtypeStruct((batch * seq, 256), F32),
        scratch_shapes=[pltpu.VMEM((SB_WIDTH, SB_HEADS * tq), F32), pltpu.VMEM((1, SB_HEADS * tq), F32)],
        compiler_params=pltpu.CompilerParams(dimension_semantics=("parallel", "parallel"),
                                             vmem_limit_bytes=VMEM_LIMIT),
        name="sb_prompt",
    )(sbqt, sbkv, sbkv_all, sbg)


def _stack_sb_heads(q):
    head = lax.broadcasted_iota(jnp.int32, q.shape, 1) // HEAD_DIM
    return jnp.concatenate([jnp.where(head == h, q, 0.0) for h in range(SB_HEADS)], axis=0).astype(BF16)


def _unstack_sb_heads(acc, t):
    head = lax.broadcasted_iota(jnp.int32, (t, SB_WIDTH), 1) // HEAD_DIM
    out = jnp.zeros((t, SB_WIDTH), F32)
    for h in range(SB_HEADS):
        out = jnp.where(head == h, acc[h * t:(h + 1) * t], out)
    return out


def _sb_sample_kernel(pt_ref, q_ref, kvn_ref, g_ref, *rest, n_pages, page):
    pages = rest[:n_pages]
    o_ref = rest[n_pages]
    new_ref = rest[n_pages + 1]
    t = q_ref.shape[0]
    rows = SB_HEADS * t
    q4 = _stack_sb_heads(q_ref[...] * SCALE)
    u = _later_matrix(page)
    trow = lax.broadcasted_iota(jnp.int32, (rows, page), 0) % t
    scol = lax.broadcasted_iota(jnp.int32, (rows, page), 1)

    new_ref[...] = jnp.zeros_like(new_ref)
    new_ref[0:t, :] = kvn_ref[...]
    mask = scol < trow
    z_new = _dot_nt(q4, new_ref[:, 0:256].astype(BF16))
    order = list(reversed(range(n_pages)))
    zs = [z_new] + [_dot(q4, pages[p][0, 0, 0:256, :].astype(BF16)) for p in order]
    log_keeps = [jnp.where(mask, -_softplus(z_new), 0.0)] + [-_softplus(z) for z in zs[1:]]
    halves = []
    for lk in log_keeps:
        halves += list(_split_bf16(lk))
    both = _dot(jnp.concatenate(halves, axis=0), u)
    run = jnp.zeros((rows, 1), F32)
    acc = jnp.zeros((rows, SB_WIDTH), F32)
    for idx, (z, lk) in enumerate(zip(zs, log_keeps)):
        later = both[2 * idx * rows:(2 * idx + 1) * rows] + both[(2 * idx + 1) * rows:(2 * idx + 2) * rows] + run
        a = jnp.exp(z + lk + later)
        if idx == 0:
            acc = acc + _dot(jnp.where(mask, a, 0.0).astype(BF16), new_ref[:, 256:512].astype(BF16))
        else:
            acc = acc + _dot_nt(a.astype(BF16), pages[order[idx - 1]][0, 0, 256:512, :].astype(BF16))
        run = run + jnp.sum(lk, axis=1, keepdims=True)
    o_ref[...] = _unstack_sb_heads(acc, t) * _silu(g_ref[...])


def _page_specs(layer, n_pages, page):
    def spec(p):
        return pl.BlockSpec((1, 1, 512, page), lambda b, pt: (layer, pt[b, p], 0, 0))
    return [spec(p) for p in range(n_pages)]


def _sb_sample(page_table, sbq, sbkv, sbg, cache_t, layer, row0, dec_batch, t):
    n_pages = page_table.shape[1]
    page = cache_t.shape[3]
    blk0 = row0 // t
    own = lambda w: pl.BlockSpec((t, w), lambda b, pt: (b, 0))
    joint = lambda w: pl.BlockSpec((t, w), lambda b, pt: (blk0 + b, 0))
    grid_spec = pltpu.PrefetchScalarGridSpec(
        num_scalar_prefetch=1,
        grid=(dec_batch,),
        in_specs=[own(256), joint(512), joint(256)] + _page_specs(layer, n_pages, page),
        out_specs=pl.BlockSpec((t, 256), lambda b, pt: (b, 0)),
        scratch_shapes=[pltpu.VMEM((page, 512), F32)],
    )
    return pl.pallas_call(
        functools.partial(_sb_sample_kernel, n_pages=n_pages, page=page),
        grid_spec=grid_spec,
        out_shape=jax.ShapeDtypeStruct((dec_batch * t, 256), F32),
        compiler_params=pltpu.CompilerParams(dimension_semantics=("parallel",), vmem_limit_bytes=VMEM_LIMIT),
        name="sb_sample",
    )(page_table, sbq, sbkv, sbg, *([cache_t] * n_pages))


def _pool_kernel(u_ref, g_ref, w_ref, sc_ref, *rest, nb, t, pos0, has_prefix):
    if has_prefix:
        pre_ref, o_ref, rows_ref = rest
    else:
        o_ref, rows_ref = rest
    pad = POOL_BUF + 1
    u = u_ref[...].reshape(nb, t, POOL_WIDTH)
    rows_ref[:, 0:pad, :] = jnp.zeros((nb, pad, POOL_WIDTH), F32)
    if has_prefix:
        rows_ref[:, 1:pad, :] = pre_ref[0]
    rows_ref[:, pad:pad + t, :] = u
    back = lambda s: rows_ref[:, pad - s:pad - s + t, :]
    sums = {}
    total = u
    done = 1
    for w in POOL_WINDOWS:
        for s in range(done, w):
            total = total + back(s)
        done = w
        sums[w] = total
    lane = lax.broadcasted_iota(jnp.int32, (nb, t, POOL_WIDTH), 2) // POOL_GDIM
    pos = pos0 + lax.broadcasted_iota(jnp.int32, (nb, t, POOL_WIDTH), 1)
    window = jnp.zeros((nb, t, POOL_WIDTH), jnp.int32)
    picked = jnp.zeros((nb, t, POOL_WIDTH), F32)
    for gi, w in enumerate(POOL_WINDOWS):
        window = jnp.where(lane == gi, w, window)
        picked = jnp.where(lane == gi, sums[w], picked)
    count = jnp.minimum(pos + 1, window).astype(F32)
    d = picked / count - u
    y = _dot(d.reshape(nb * t, POOL_WIDTH).astype(BF16), w_ref[...]) * sc_ref[...]
    o_ref[...] = y * _silu(g_ref[...])


def _pool(pu, pg, w_bd, scale, prefix, layer, row0, nb, t, n_seq, pos0):
    rows = nb * t
    blk0 = row0 // rows
    has_prefix = prefix is not None
    in_specs = [pl.BlockSpec((rows, 256), lambda i: (blk0 + i, 0)),
                pl.BlockSpec((rows, 256), lambda i: (blk0 + i, 0)),
                pl.BlockSpec((256, 256), lambda i: (0, 0)),
                pl.BlockSpec((1, 256), lambda i: (0, 0))]
    args = [pu, pg, w_bd, scale]
    if has_prefix:
        in_specs.append(pl.BlockSpec((1, nb, POOL_BUF, 256), lambda i: (layer, i, 0, 0)))
        args.append(prefix)
    return pl.pallas_call(
        functools.partial(_pool_kernel, nb=nb, t=t, pos0=pos0, has_prefix=has_prefix),
        grid=(n_seq // nb,),
        in_specs=in_specs,
        out_specs=pl.BlockSpec((rows, 256), lambda i: (i, 0)),
        out_shape=jax.ShapeDtypeStruct((n_seq * t, 256), F32),
        scratch_shapes=[pltpu.VMEM((nb, POOL_BUF + 1 + t, 256), F32)],
        compiler_params=pltpu.CompilerParams(dimension_semantics=("parallel",), vmem_limit_bytes=VMEM_LIMIT),
        name="pool_sample" if has_prefix else "pool_prompt",
    )(*args)


def _stack_nsa_heads_t(ref, t):
    low = lax.broadcasted_iota(jnp.int32, (LANES, t), 0) < HEAD_DIM
    g0, g1 = [], []
    for c in range(NSA_GROUP):
        ch = ref[c * LANES:(c + 1) * LANES, :] * SCALE
        g0.append(jnp.where(low, ch, 0.0))
        g1.append(jnp.where(low, 0.0, ch))
    return jnp.concatenate(g0 + g1, axis=1).astype(BF16)


def _per_group_lanes(x, t):
    a, b = x[:, :t], x[:, t:]
    return jnp.concatenate([a] * NSA_GROUP + [b] * NSA_GROUP, axis=1)


def _select_blocks_t(imp, qpos, n_sel):
    blk = lax.broadcasted_iota(jnp.int32, imp.shape, 0)
    forced = (blk == qpos // SEL_BLOCK) | (blk == 0)
    future = blk * SEL_BLOCK > qpos
    imp = jnp.where(forced, FORCED_SCORE, jnp.where(future, -1.0, imp))
    rank = jnp.zeros(imp.shape, jnp.int32)
    for i in range(n_sel):
        row = imp[i:i + 1, :]
        ahead = (row > imp) | ((row == imp) & (blk > i))
        rank = rank + ahead.astype(jnp.int32)
    return (rank < min(SEL_TOPN, n_sel)).astype(F32)


def _nsa_prompt_kernel(nqct_ref, nqrt_ref, bgt_ref, ng_ref, cmp_ref, ks_ref, vst_ref, wk_ref, wvt_ref,
                       o_ref, m_ref, l_ref, acc_ref):
    tq = nqct_ref.shape[1]
    tk = K_TILE
    n_sel = cmp_ref.shape[0]
    i = pl.program_id(1)
    q0 = i * tq
    n_rows = NSA_HEADS * tq
    qct = _stack_nsa_heads_t(nqct_ref, tq)
    qrt = _stack_nsa_heads_t(nqrt_ref, tq)
    qpos = q0 + lax.broadcasted_iota(jnp.int32, (1, n_rows), 1) % tq
    qpos2 = q0 + lax.broadcasted_iota(jnp.int32, (1, NSA_KV_HEADS * tq), 1) % tq

    s_e = _dot(cmp_ref[:, 0:128].astype(BF16), qct)
    s_o = _dot(cmp_ref[:, 128:256].astype(BF16), qct)
    blk = lax.broadcasted_iota(jnp.int32, (n_sel, n_rows), 0)
    s_e = jnp.where(blk * SEL_BLOCK + (CMP_BLOCK - 1) <= qpos, s_e, NEG_INF)
    s_o = jnp.where(blk * SEL_BLOCK + (SEL_BLOCK - 1) <= qpos, s_o, NEG_INF)
    m = jnp.maximum(jnp.max(s_e, axis=0, keepdims=True), jnp.max(s_o, axis=0, keepdims=True))
    m = jnp.where(m == NEG_INF, 0.0, m)
    p_e = jnp.exp(s_e - m)
    p_o = jnp.exp(s_o - m)
    denom = jnp.maximum(jnp.sum(p_e, axis=0, keepdims=True) + jnp.sum(p_o, axis=0, keepdims=True), 1e-30)
    p_e = p_e / denom
    p_o = p_o / denom
    o_c = (_dot_tn(cmp_ref[:, 256:384].astype(BF16), p_e.astype(BF16))
           + _dot_tn(cmp_ref[:, 384:512].astype(BF16), p_o.astype(BF16)))
    pair = p_e + p_o
    imp = []
    for g in range(NSA_KV_HEADS):
        acc = pair[:, (g * NSA_GROUP) * tq:(g * NSA_GROUP + 1) * tq]
        for c in range(1, NSA_GROUP):
            acc = acc + pair[:, (g * NSA_GROUP + c) * tq:(g * NSA_GROUP + c + 1) * tq]
        imp.append(acc)
    sel = _select_blocks_t(jnp.concatenate(imp, axis=1), qpos2, n_sel)

    def reset():
        m_ref[...] = jnp.full_like(m_ref, NEG_INF)
        l_ref[...] = jnp.zeros_like(l_ref)
        acc_ref[...] = jnp.zeros_like(acc_ref)

    def online_softmax(s, vt):
        m_old = m_ref[...]
        m_new = jnp.maximum(m_old, jnp.max(s, axis=0, keepdims=True))
        p = jnp.exp(s - m_new)
        alpha = jnp.exp(m_old - m_new)
        l_ref[...] = alpha * l_ref[...] + jnp.sum(p, axis=0, keepdims=True)
        acc_ref[...] = alpha * acc_ref[...] + _dot(vt, p.astype(BF16))
        m_ref[...] = m_new

    def finish():
        return acc_ref[...] / jnp.maximum(l_ref[...], 1e-30)

    krow = lax.broadcasted_iota(jnp.int32, (tk, n_rows), 0)
    last = (q0 + tq - 1) // tk

    member_bias = _per_group_lanes((sel - 1.0) * (-MASKED), tq).astype(BF16)
    q_aug = jnp.concatenate([qrt, member_bias], axis=0)
    reset()

    def slc_step(j, diagonal):
        start = pl.multiple_of(j * tk, tk)
        key_blk = (j * tk + lax.broadcasted_iota(jnp.int32, (tk, n_sel), 0)) // SEL_BLOCK
        onehot = (key_blk == lax.broadcasted_iota(jnp.int32, (tk, n_sel), 1)).astype(BF16)
        k_aug = jnp.concatenate([ks_ref[pl.ds(start, tk), :].astype(BF16), onehot], axis=1)
        s = _dot(k_aug, q_aug)
        if diagonal:
            s = jnp.where(krow + j * tk <= qpos, s, MASKED)
        online_softmax(s, vst_ref[0, 0, :, pl.ds(start, tk)].astype(BF16))

    def slc_body(j, carry):
        slc_step(j, False)
        return carry

    lax.fori_loop(0, last, slc_body, 0)
    slc_step(last, True)
    o_s = finish()

    reset()

    def win_body(j, carry):
        start = pl.multiple_of(j * tk, tk)
        s = _dot(wk_ref[pl.ds(start, tk), :].astype(BF16), qrt)
        dist = qpos - (krow + j * tk)
        s = jnp.where((dist >= 0) & (dist < WINDOW), s, MASKED)
        online_softmax(s, wvt_ref[0, :, pl.ds(start, tk)].astype(BF16))
        return carry

    lax.fori_loop(jnp.maximum(q0 - WINDOW + 1, 0) // tk, last + 1, win_body, 0)
    o_w = finish()

    gate = 1.0 / (1.0 + jnp.exp(-bgt_ref[...]))
    low = lax.broadcasted_iota(jnp.int32, (LANES, tq), 0) < HEAD_DIM
    for c in range(NSA_GROUP):
        per_group = []
        for g in range(NSA_KV_HEADS):
            hd = g * NSA_GROUP + c
            sl = slice(hd * tq, (hd + 1) * tq)
            per_group.append(gate[3 * hd:3 * hd + 1, :] * o_c[:, sl] + gate[3 * hd + 1:3 * hd + 2, :] * o_s[:, sl]
                             + gate[3 * hd + 2:3 * hd + 3, :] * o_w[:, sl])
        mixed = jnp.where(low, per_group[0], per_group[1])
        cs = slice(c * LANES, (c + 1) * LANES)
        o_ref[:, cs] = mixed.T * _silu(ng_ref[:, cs])


def _nsa_prompt(nqct, nqrt, bgt, ngate, cmp, nsa_new, nsa_all, win_new, win_t, layer, batch, seq):
    tq = Q_TILE
    nq = seq // tq
    n_sel = seq // SEL_BLOCK
    qcol = lambda f: pl.BlockSpec((f, tq), lambda b, i: (0, b * nq + i))
    qrow = lambda w: pl.BlockSpec((tq, w), lambda b, i: (b * nq + i, 0))
    n_rows = NSA_HEADS * tq
    return pl.pallas_call(
        _nsa_prompt_kernel,
        grid=(batch, nq),
        in_specs=[qcol(512), qcol(512), qcol(128), qrow(512),
                  pl.BlockSpec((n_sel, 512), lambda b, i: (b, 0)),
                  pl.BlockSpec((seq, LANES), lambda b, i: (b, 2)),
                  pl.BlockSpec((1, 1, LANES, seq), lambda b, i: (layer, b, 3, 0)),
                  pl.BlockSpec((seq, LANES), lambda b, i: (b, 0)),
                  pl.BlockSpec((1, LANES, seq), lambda b, i: (b, 1, 0))],
        out_specs=qrow(512),
        out_shape=jax.ShapeDtypeStruct((batch * seq, 512), F32),
        scratch_shapes=[pltpu.VMEM((1, n_rows), F32), pltpu.VMEM((1, n_rows), F32), pltpu.VMEM((LANES, n_rows), F32)],
        compiler_params=pltpu.CompilerParams(dimension_semantics=("parallel", "parallel"),
                                             vmem_limit_bytes=VMEM_LIMIT),
        name="nsa_prompt",
    )(nqct, nqrt, bgt, ngate, cmp, nsa_new, nsa_all, win_new, win_t)


def _stack_nsa_heads(ref, t):
    low = lax.broadcasted_iota(jnp.int32, (t, LANES), 1) < HEAD_DIM
    g0, g1 = [], []
    for c in range(NSA_GROUP):
        ch = ref[:, c * LANES:(c + 1) * LANES] * SCALE
        g0.append(jnp.where(low, ch, 0.0))
        g1.append(jnp.where(low, 0.0, ch))
    return jnp.concatenate(g0 + g1, axis=0).astype(BF16)


def _per_group_rows(x, t):
    a, b = x[:t], x[t:]
    return jnp.concatenate([a] * NSA_GROUP + [b] * NSA_GROUP, axis=0)


def _select_blocks(imp, qpos, n_sel):
    blk = lax.broadcasted_iota(jnp.int32, imp.shape, 1)
    forced = (blk == qpos // SEL_BLOCK) | (blk == 0)
    future = blk * SEL_BLOCK > qpos
    imp = jnp.where(forced, FORCED_SCORE, jnp.where(future, -1.0, imp))
    rank = jnp.zeros(imp.shape, jnp.int32)
    for i in range(n_sel):
        col = imp[:, i:i + 1]
        ahead = (col > imp) | ((col == imp) & (blk > i))
        rank = rank + ahead.astype(jnp.int32)
    return ((rank < min(SEL_TOPN, n_sel)) & (blk < n_sel)).astype(BF16)


def _expand_matrix(n_rows, n_keys, key0):
    r = lax.broadcasted_iota(jnp.int32, (n_rows, n_keys), 0)
    c = lax.broadcasted_iota(jnp.int32, (n_rows, n_keys), 1)
    return ((key0 + c) // SEL_BLOCK == r).astype(BF16)


def _softmax_parts(parts):
    m = None
    for s, _, _ in parts:
        pm = jnp.max(s, axis=1, keepdims=True)
        m = pm if m is None else jnp.maximum(m, pm)
    m = jnp.where(m == NEG_INF, 0.0, m)
    denom = None
    acc = None
    for s, v, feature_major in parts:
        p = jnp.exp(s - m)
        ps = jnp.sum(p, axis=1, keepdims=True)
        pv = _dot_nt(p.astype(BF16), v) if feature_major else _dot(p.astype(BF16), v)
        denom = ps if denom is None else denom + ps
        acc = pv if acc is None else acc + pv
    return acc / jnp.maximum(denom, 1e-30)


def _nsa_sample_kernel(pt_ref, nqc_ref, nqr_ref, new_ref, winn_ref, bg_ref, ng_ref, swin_ref, phi_ref, seg_ref,
                       *rest, n_pages, page, pos0):
    pages = rest[:n_pages]
    o_ref, ckall_ref, cvall_ref, newpad_ref, winpad_ref = rest[n_pages:]
    t = nqc_ref.shape[0]
    rows = NSA_HEADS * t
    n_sel = (pos0 + t + SEL_BLOCK - 1) // SEL_BLOCK
    qc = _stack_nsa_heads(nqc_ref, t)
    qr = _stack_nsa_heads(nqr_ref, t)
    trow = lax.broadcasted_iota(jnp.int32, (rows, 1), 0) % t
    trow2 = lax.broadcasted_iota(jnp.int32, (NSA_KV_HEADS * t, 1), 0) % t
    qpos, qpos2 = pos0 + trow, pos0 + trow2

    for p in range(n_pages):
        ckall_ref[:, p * page:(p + 1) * page] = pages[p][0, 0, 0:128, :] * phi_ref[0:1, :]
        cvall_ref[:, p * page:(p + 1) * page] = pages[p][0, 0, 128:256, :] * phi_ref[1:2, :]
    seg = seg_ref[...]
    cmp_t = []
    for src_ref in (ckall_ref, cvall_ref):
        hi, lo = _split_bf16(src_ref[...])
        both = _dot(jnp.concatenate([hi, lo], axis=0), seg)
        cmp_t.append((both[:LANES] + both[LANES:]).astype(BF16))
    kcmp_t, vcmp_t = cmp_t
    s = _dot(qc, kcmp_t)
    col = lax.broadcasted_iota(jnp.int32, (rows, LANES), 1)
    block_end = jnp.where(col < SEL_BLOCK, col * SEL_BLOCK + (CMP_BLOCK - 1),
                          (col - SEL_BLOCK) * SEL_BLOCK + (SEL_BLOCK - 1))
    s = jnp.where(block_end <= qpos, s, NEG_INF)
    m = jnp.max(s, axis=1, keepdims=True)
    m = jnp.where(m == NEG_INF, 0.0, m)
    p_c = jnp.exp(s - m)
    p_c = p_c / jnp.maximum(jnp.sum(p_c, axis=1, keepdims=True), 1e-30)
    o_c = _dot_nt(p_c.astype(BF16), vcmp_t)
    imp = []
    for g in range(NSA_KV_HEADS):
        acc = p_c[(g * NSA_GROUP) * t:(g * NSA_GROUP + 1) * t]
        for c in range(1, NSA_GROUP):
            acc = acc + p_c[(g * NSA_GROUP + c) * t:(g * NSA_GROUP + c + 1) * t]
        imp.append(acc)
    imp = jnp.concatenate(imp, axis=0)
    imp = imp + pltpu.roll(imp, SEL_BLOCK, 1)
    sel = _select_blocks(imp, qpos2, n_sel)

    newpad_ref[...] = jnp.zeros_like(newpad_ref)
    newpad_ref[0:t, :] = new_ref[...]
    winpad_ref[...] = jnp.zeros_like(winpad_ref)
    winpad_ref[0:t, :] = winn_ref[...]
    scol = lax.broadcasted_iota(jnp.int32, (NSA_KV_HEADS * t, page), 1)

    member = _dot(sel, _expand_matrix(LANES, n_pages * page, 0))
    bias_past = _per_group_rows(jnp.where(member > 0.5, 0.0, NEG_INF), t)
    parts = []
    for p in range(n_pages):
        sc = _dot(qr, pages[p][0, 0, 256:384, :].astype(BF16)) + bias_past[:, p * page:(p + 1) * page]
        parts.append((sc, pages[p][0, 0, 384:512, :].astype(BF16), True))
    member_new = _dot(sel, _expand_matrix(LANES, page, n_pages * page))
    ok_new = (member_new > 0.5) & (scol <= trow2) & (scol < t)
    bias_new = _per_group_rows(jnp.where(ok_new, 0.0, NEG_INF), t)
    parts.append((_dot_nt(qr, newpad_ref[:, 256:384].astype(BF16)) + bias_new,
                  newpad_ref[:, 384:512].astype(BF16), False))
    o_s = _softmax_parts(parts)

    wlen = swin_ref.shape[3]
    rcol = lax.broadcasted_iota(jnp.int32, (NSA_KV_HEADS * t, wlen), 1)
    dist = wlen + trow2 - rcol
    bias_w = _per_group_rows(jnp.where((dist >= 0) & (dist < WINDOW), 0.0, NEG_INF), t)
    dist_n = trow2 - scol
    bias_wn = _per_group_rows(jnp.where((dist_n >= 0) & (dist_n < WINDOW) & (scol < t), 0.0, NEG_INF), t)
    parts = [(_dot(qr, swin_ref[0, 0, 0:128, :].astype(BF16)) + bias_w, swin_ref[0, 0, 128:256, :].astype(BF16), True),
             (_dot_nt(qr, winpad_ref[:, 0:128].astype(BF16)) + bias_wn, winpad_ref[:, 128:256].astype(BF16), False)]
    o_w = _softmax_parts(parts)

    gate = 1.0 / (1.0 + jnp.exp(-bg_ref[...]))
    low = lax.broadcasted_iota(jnp.int32, (t, LANES), 1) < HEAD_DIM
    for c in range(NSA_GROUP):
        per_group = []
        for g in range(NSA_KV_HEADS):
            r0 = (g * NSA_GROUP + c) * t
            cg = (g * NSA_GROUP + c) * 3
            per_group.append(gate[:, cg:cg + 1] * o_c[r0:r0 + t] + gate[:, cg + 1:cg + 2] * o_s[r0:r0 + t]
                             + gate[:, cg + 2:cg + 3] * o_w[r0:r0 + t])
        mixed = jnp.where(low, per_group[0], per_group[1])
        sl = slice(c * LANES, (c + 1) * LANES)
        o_ref[:, sl] = mixed * _silu(ng_ref[:, sl])


def _nsa_sample(page_table, nqc, nqr, nsa_new, win_new, bgate, ngate, state_win_t, phi_rows, seg, cache_t,
                layer, row0, dec_batch, t, pos0):
    n_pages = page_table.shape[1]
    page = cache_t.shape[3]
    blk0 = row0 // t
    own = lambda w: pl.BlockSpec((t, w), lambda b, pt: (b, 0))
    joint = lambda w: pl.BlockSpec((t, w), lambda b, pt: (blk0 + b, 0))
    wlen = state_win_t.shape[3]
    n_keys = n_pages * page
    grid_spec = pltpu.PrefetchScalarGridSpec(
        num_scalar_prefetch=1,
        grid=(dec_batch,),
        in_specs=[own(512), own(512), joint(512), joint(256), own(128), joint(512),
                  pl.BlockSpec((1, 1, 256, wlen), lambda b, pt: (layer, b, 0, 0)),
                  pl.BlockSpec((2, page), lambda b, pt: (0, 0)),
                  pl.BlockSpec((n_keys, LANES), lambda b, pt: (0, 0))]
        + _page_specs(layer, n_pages, page),
        out_specs=pl.BlockSpec((t, 512), lambda b, pt: (b, 0)),
        scratch_shapes=[pltpu.VMEM((LANES, n_keys), F32), pltpu.VMEM((LANES, n_keys), F32),
                        pltpu.VMEM((page, 512), F32), pltpu.VMEM((page, 256), F32)],
    )
    return pl.pallas_call(
        functools.partial(_nsa_sample_kernel, n_pages=n_pages, page=page, pos0=pos0),
        grid_spec=grid_spec,
        out_shape=jax.ShapeDtypeStruct((dec_batch * t, 512), F32),
        compiler_params=pltpu.CompilerParams(dimension_semantics=("parallel",), vmem_limit_bytes=VMEM_LIMIT),
        name="nsa_sample",
    )(page_table, nqc, nqr, nsa_new, win_new, bgate, ngate, state_win_t, phi_rows, seg, *([cache_t] * n_pages))


def _out_proj_kernel(oap_ref, obp_ref, ocp_ref, oas_ref, obs_ref, ocs_ref, x_ref, w_ref, g_ref, y_ref, *,
                     n_prompt_tiles):
    prompt = pl.program_id(0) < n_prompt_tiles
    pick = lambda p_ref, s_ref: jnp.where(prompt, p_ref[...], s_ref[...]).astype(BF16)
    out = (_dot(pick(oap_ref, oas_ref), w_ref[0:256, :])
           + _dot(pick(obp_ref, obs_ref), w_ref[256:512, :])
           + _dot(pick(ocp_ref, ocs_ref), w_ref[512:1024, :]))
    ms = jnp.mean(out * out, axis=-1, keepdims=True)
    y_ref[...] = x_ref[...] + out * lax.rsqrt(ms + NORM_EPS) * g_ref[...]


def _out_proj(prompt_parts, sample_parts, x, w_out_p, g_post, n_p):
    n = x.shape[0]
    tm = ROW_TILE
    npt = n_p // tm
    row = lambda w: pl.BlockSpec((tm, w), lambda i: (i, 0))
    prompt_row = lambda w: pl.BlockSpec((tm, w), lambda i: (jnp.minimum(i, npt - 1), 0))
    sample_row = lambda w: pl.BlockSpec((tm, w), lambda i: (jnp.maximum(i - npt, 0), 0))
    fixed = lambda s: pl.BlockSpec(s, lambda i: (0, 0), pipeline_mode=pl.Buffered(1))
    widths = (256, 256, 512)
    return pl.pallas_call(
        functools.partial(_out_proj_kernel, n_prompt_tiles=npt),
        grid=(n // tm,),
        in_specs=[prompt_row(w) for w in widths] + [sample_row(w) for w in widths]
        + [row(D_MODEL), fixed((D_MODEL, D_MODEL)), fixed((1, D_MODEL))],
        out_specs=row(D_MODEL),
        out_shape=jax.ShapeDtypeStruct((n, D_MODEL), F32),
        compiler_params=pltpu.CompilerParams(dimension_semantics=("arbitrary",), vmem_limit_bytes=VMEM_LIMIT),
        name="out_proj",
    )(*prompt_parts, *sample_parts, x, w_out_p, g_post)


def _win_update_kernel(state_ref, new_ref, o_ref, *, nb, t):
    wlen = state_ref.shape[3]
    for bi in range(nb):
        o_ref[0, bi] = pltpu.roll(state_ref[0, bi], wlen - t, 1)
        o_ref[0, bi, :, wlen - t:wlen] = new_ref[0, bi * t:(bi + 1) * t, :].T


def _win_update(state_win_t, win_rows, t):
    depth, bs, width, wlen = state_win_t.shape
    nb = 8
    return pl.pallas_call(
        functools.partial(_win_update_kernel, nb=nb, t=t),
        grid=(depth, bs // nb),
        in_specs=[pl.BlockSpec((1, nb, width, wlen), lambda l, b: (l, b, 0, 0)),
                  pl.BlockSpec((1, nb * t, width), lambda l, b: (l, b, 0))],
        out_specs=pl.BlockSpec((1, nb, width, wlen), lambda l, b: (l, b, 0, 0)),
        out_shape=jax.ShapeDtypeStruct(state_win_t.shape, F32),
        compiler_params=pltpu.CompilerParams(dimension_semantics=("parallel", "parallel"),
                                             vmem_limit_bytes=VMEM_LIMIT),
        name="win_update",
    )(state_win_t, win_rows)


def _to_chunk_layout(w, axis):
    shape = w.shape
    split = shape[:axis] + (NSA_KV_HEADS, NSA_GROUP, HEAD_DIM) + shape[axis + 1:]
    return jnp.swapaxes(w.reshape(split), axis, axis + 1).reshape(shape)


def _rope_tables(pos):
    inv = ROPE_THETA ** (-jnp.arange(ROT_HALF, dtype=F32) * (2.0 / ROT_DIM))
    ang = pos.astype(F32)[:, None] * inv[None, :]
    cos, sin = jnp.cos(ang), jnp.sin(ang)
    n = pos.shape[0]
    cos_h = jnp.concatenate([cos, cos, jnp.ones((n, HEAD_DIM - ROT_DIM), F32)], axis=1)
    sin_h = jnp.concatenate([-sin, sin, jnp.zeros((n, HEAD_DIM - ROT_DIM), F32)], axis=1)
    reps = (1, LANES // HEAD_DIM)
    return jnp.tile(cos_h, reps), jnp.tile(sin_h, reps), cos.T, sin.T


def _segment_matrix(n_keys):
    n = np.arange(n_keys) // CMP_BLOCK
    col = (n % 2) * SEL_BLOCK + n // 2
    return jnp.asarray(col[:, None] == np.arange(LANES)[None, :], BF16)


def kernel(x_prompt, x_sample, cache_sb_kv, cache_nsa_kv, state_win_kv, state_pool, page_table, norm_pre, w_in,
           pool_w, pool_scale, phi_k, phi_v, w_out, norm_post):
    bp, tp, d = x_prompt.shape
    bs, ts, _ = x_sample.shape
    depth = w_in.shape[0]
    n_pool, page = cache_sb_kv.shape[1], cache_sb_kv.shape[2]
    past_len = page_table.shape[1] * page
    wlen = state_win_kv.shape[2]
    n_p, n_s = bp * tp, bs * ts
    assert d == D_MODEL and n_p % ROW_TILE == 0 and n_s % ROW_TILE == 0
    assert tp % K_TILE == 0 and K_TILE % Q_TILE == 0 and tp % SEL_BLOCK == 0 and n_p % n_s == 0
    assert past_len % SEL_BLOCK == 0 and ts < CMP_BLOCK and ts % 8 == 0 and page == LANES
    assert past_len // CMP_BLOCK <= SEL_BLOCK and (wlen >= WINDOW or wlen == past_len)

    o = np.cumsum((0,) + (256,) * 6 + (512,) + (128,) * 6 + (24, 512)).tolist()
    w_rows = jnp.swapaxes(w_in, 1, 2)
    w_t = jnp.concatenate([
        w_rows[:, o[0]:o[6]], _to_chunk_layout(w_rows[:, o[6]:o[7]], 1), w_rows[:, o[7]:o[13]],
        _to_chunk_layout(w_rows[:, o[14]:o[15]], 1), w_rows[:, o[13]:o[14]],
        jnp.zeros((depth, IN_WIDTH_P - o[15], D_MODEL), w_in.dtype)], axis=1).astype(BF16)
    w_out_p = jnp.concatenate([w_out[:, :512], _to_chunk_layout(w_out[:, 512:], 1)], axis=1).astype(BF16)
    w_pool = jnp.zeros((depth, POOL_WIDTH, POOL_WIDTH), F32)
    for g in range(len(POOL_WINDOWS)):
        sl = slice(g * POOL_GDIM, (g + 1) * POOL_GDIM)
        w_pool = w_pool.at[:, sl, sl].set(pool_w[:, g])
    w_pool = w_pool.astype(BF16)
    reps = page // CMP_BLOCK
    phi_rows = jnp.stack([jnp.tile(phi_k, (1, reps)), jnp.tile(phi_v, (1, reps))], axis=1)
    seg = _segment_matrix(past_len)

    pos = jnp.concatenate([jnp.tile(jnp.arange(tp, dtype=jnp.int32), bp),
                           jnp.tile(past_len + jnp.arange(ts, dtype=jnp.int32), bs)])
    tables = _rope_tables(pos)

    cache_sb_t = jnp.transpose(cache_sb_kv, (0, 1, 3, 4, 5, 2)).reshape(depth, n_pool, 512, page)
    cache_nsa_t = jnp.transpose(cache_nsa_kv, (0, 1, 3, 4, 5, 2)).reshape(depth, n_pool, 512, page)
    state_win_t = jnp.transpose(state_win_kv, (0, 1, 3, 4, 5, 2)).reshape(depth, bs, 256, wlen)

    x = jnp.concatenate([x_prompt.reshape(n_p, d), x_sample.reshape(n_s, d)], axis=0)
    sbkv_all = jnp.zeros((depth, bp, 512, tp), F32)
    nsa_all = jnp.zeros((depth, bp, 512, tp), F32)
    keep_p = min(WINDOW, tp)
    win_p, pool_p, sb_s, nsa_s, win_rows, pool_s = [], [], [], [], [], []
    for layer in range(depth):
        (sbkv, sbg, pu, pg, nsa_new, win_new, ngate, cmp, sbq, nqc, nqr, bgate,
         sbqt, nqct, nqrt, bgt, sbkv_all, nsa_all, win_t) = _in_proj(
            x, norm_pre[layer][None], w_t[layer], phi_k[layer], phi_v[layer], tables, sbkv_all, nsa_all,
            layer, n_p, bp, tp)
        pool_args = (pu, pg, w_pool[layer], pool_scale[layer][None])
        prompt_parts = (
            _sb_prompt(sbqt, sbkv, sbkv_all, sbg, layer, bp, tp),
            _pool(*pool_args, None, layer, 0, 1, tp, bp, 0),
            _nsa_prompt(nqct, nqrt, bgt, ngate, cmp, nsa_new, nsa_all, win_new, win_t, layer, bp, tp))
        sample_parts = (
            _sb_sample(page_table, sbq, sbkv, sbg, cache_sb_t, layer, n_p, bs, ts),
            _pool(*pool_args, state_pool, layer, n_p, bs, ts, bs, past_len),
            _nsa_sample(page_table, nqc, nqr, nsa_new, win_new, bgate, ngate, state_win_t, phi_rows[layer], seg,
                        cache_nsa_t, layer, n_p, bs, ts, past_len))
        x = _out_proj(prompt_parts, sample_parts, x, w_out_p[layer], norm_post[layer][None], n_p)

        win_p.append(win_t[:, :, tp - keep_p:])
        pool_p.append(pu[:n_p].reshape(bp, tp, POOL_WIDTH)[:, tp - POOL_BUF:])
        sb_s.append(sbkv[n_p:].reshape(bs, ts, 2, SB_HEADS, HEAD_DIM))
        nsa_s.append(nsa_new[n_p:].reshape(bs, ts, 4, NSA_KV_HEADS, HEAD_DIM))
        win_rows.append(win_new[n_p:])
        pool_s.append(jnp.concatenate([state_pool[layer], pu[n_p:].reshape(bs, ts, 256)], axis=1)[:, -POOL_BUF:])
    win_s_t = _win_update(state_win_t, jnp.stack(win_rows), ts)

    def token_major(a, n_kv, n_heads):
        lead = a.shape[:-2]
        a = a.reshape(lead + (n_kv, n_heads, HEAD_DIM, a.shape[-1]))
        return jnp.moveaxis(a, -1, len(lead))

    y_prompt = x[:n_p].reshape(bp, tp, d)
    y_sample = x[n_p:].reshape(bs, ts, d)
    return (y_prompt, y_sample,
            token_major(sbkv_all, 2, SB_HEADS), token_major(nsa_all, 4, NSA_KV_HEADS),
            token_major(jnp.stack(win_p), 2, NSA_KV_HEADS), jnp.stack(pool_p),
            jnp.stack(sb_s), jnp.stack(nsa_s), token_major(win_s_t, 2, NSA_KV_HEADS), jnp.stack(pool_s))
```

```python
import functools

import jax
import jax.numpy as jnp
import numpy as np
from jax import lax
from jax.experimental import pallas as pl
from jax.experimental.pallas import tpu as pltpu

F32 = jnp.float32
BF16 = jnp.bfloat16

D_MODEL = 1024
HEAD_DIM = 64
SB_WIDTH = 256
SB_HEADS = 4
POOL_WINDOWS = (2, 4, 8, 16)
POOL_WIDTH = 256
POOL_GDIM = 64
POOL_BUF = 15
NSA_WIDTH = 512
NSA_HEADS = 8
NSA_KV_HEADS = 2
NSA_GROUP = 4
NSA_KV_WIDTH = 128
CMP_BLOCK = 32
SEL_BLOCK = 64
SEL_TOPN = 8
WINDOW = 512
ROPE_THETA = 500000.0
ROT_DIM = 16
ROT_HALF = ROT_DIM // 2
NORM_EPS = 1e-6
FORCED_SCORE = 1e4
SCALE = HEAD_DIM ** -0.5
LOG2E = 1.4426950408889634
NEG_INF = float("-inf")
MASKED = -(2.0 ** 100)

C_SBQ, C_NQ, C_BG, C_QEND = 0, 256, 768, 896
C_SBK, C_SBG, C_PU, C_PG = 896, 1408, 1664, 1920
C_NSA = 2176
C_WIN = 2688
C_NG = 2944
IN_WIDTH_P = 3456
LANES = 128
ROW_TILE = 512
Q_TILE = 128
K_TILE = 256
VMEM_LIMIT = 56 * 1024 * 1024


def _dot(a, b):
    return jnp.dot(a, b, preferred_element_type=F32)


def _dot_nt(a, b):
    return lax.dot_general(a, b, (((1,), (1,)), ((), ())), preferred_element_type=F32)


def _dot_tn(a, b):
    return lax.dot_general(a, b, (((0,), (0,)), ((), ())), preferred_element_type=F32)


def _silu(g):
    return g / (1.0 + jnp.exp(-g))


def _softplus(z):
    return jnp.maximum(z, 0.0) + jnp.log(1.0 + jnp.exp(-jnp.abs(z)))


def _split_bf16(x):
    hi = x.astype(BF16)
    return hi, (x - hi.astype(F32)).astype(BF16)


def _suffix_sum_exclusive(x, u_bf16):
    hi, lo = _split_bf16(x)
    n = x.shape[0]
    both = _dot(jnp.concatenate([hi, lo], axis=0), u_bf16)
    return both[:n] + both[n:]


def _later_matrix(n, transposed=False):
    r = lax.broadcasted_iota(jnp.int32, (n, n), 0)
    c = lax.broadcasted_iota(jnp.int32, (n, n), 1)
    return ((c > r) if transposed else (r > c)).astype(BF16)


def _in_proj_kernel(phik_ref, phiv_ref, x_ref, g_ref, wt_ref, cos_ref, sin_ref, cost_ref, sint_ref,
                    sbkv_prev_ref, nsa_prev_ref,
                    sbkv_ref, sbg_ref, pu_ref, pg_ref, nsa_ref, win_ref, ng_ref, cmp_ref,
                    sbq_ref, nqc_ref, nqr_ref, bg_ref,
                    sbqt_ref, nqct_ref, nqrt_ref, bgt_ref, sbkvt_ref, nsat_ref, wint_ref,
                    ck_ref, cv_ref, *, n_prompt_tiles):
    del sbkv_prev_ref, nsa_prev_ref
    i = pl.program_id(0)
    x = x_ref[...]
    ms = jnp.mean(x * x, axis=-1, keepdims=True)
    h = (x * lax.rsqrt(ms + NORM_EPS) * g_ref[...]).astype(BF16)

    def proj(a, b):
        return _dot_nt(h, wt_ref[a:b, :])

    cos = cos_ref[...]
    sin = sin_ref[...]
    first = (lax.broadcasted_iota(jnp.int32, cos.shape, 1) % HEAD_DIM) < ROT_HALF

    def rope(v):
        swapped = jnp.where(first, pltpu.roll(v, LANES - ROT_HALF, 1), pltpu.roll(v, ROT_HALF, 1))
        return v * cos + swapped * sin

    sbkv_ref[...] = proj(C_SBK, C_SBK + 512)
    gates = proj(C_SBG, C_SBG + 768)
    sbg_ref[...] = gates[:, 0:256]
    pu_ref[...] = gates[:, 256:512]
    pg_ref[...] = gates[:, 512:768]
    ng_ref[...] = proj(C_NG, C_NG + 512)
    nsa = proj(C_NSA, C_NSA + 512)
    ck_ref[...] = nsa[:, 0:128]
    cv_ref[...] = nsa[:, 128:256]
    nsa_ref[:, 0:256] = nsa[:, 0:256]
    nsa_ref[:, 256:384] = rope(nsa[:, 256:384])
    nsa_ref[:, 384:512] = nsa[:, 384:512]
    win = proj(C_WIN, C_WIN + 256)
    win_ref[:, 0:128] = rope(win[:, 0:128])
    win_ref[:, 128:256] = win[:, 128:256]

    nb = cmp_ref.shape[0]
    for part, (src_ref, phi_ref) in enumerate(((ck_ref, phik_ref), (cv_ref, phiv_ref))):
        acc_e = jnp.zeros((nb, LANES), F32)
        acc_o = jnp.zeros((nb, LANES), F32)
        for l in range(CMP_BLOCK):
            acc_e = acc_e + src_ref[pl.ds(l, nb, stride=SEL_BLOCK), :] * phi_ref[l]
            acc_o = acc_o + src_ref[pl.ds(CMP_BLOCK + l, nb, stride=SEL_BLOCK), :] * phi_ref[l]
        cmp_ref[:, part * 256:part * 256 + 128] = acc_e
        cmp_ref[:, part * 256 + 128:part * 256 + 256] = acc_o

    @pl.when(i >= n_prompt_tiles)
    def _():
        q_all = proj(C_SBQ, C_QEND)
        sbq_ref[...] = q_all[:, C_SBQ:C_SBQ + 256]
        bg_ref[...] = q_all[:, C_BG:C_BG + 128]
        nqc_ref[...] = q_all[:, C_NQ:C_NQ + 512]
        for c in range(NSA_GROUP):
            nqr_ref[:, c * LANES:(c + 1) * LANES] = rope(q_all[:, C_NQ + c * LANES:C_NQ + (c + 1) * LANES])

    @pl.when(i < n_prompt_tiles)
    def _():
        cos_t = cost_ref[...]
        sin_t = sint_ref[...]

        def rope_t(v):
            pieces = []
            for base in range(0, LANES, HEAD_DIM):
                x1 = v[base:base + ROT_HALF]
                x2 = v[base + ROT_HALF:base + ROT_DIM]
                pieces += [x1 * cos_t - x2 * sin_t, x2 * cos_t + x1 * sin_t, v[base + ROT_DIM:base + HEAD_DIM]]
            return jnp.concatenate(pieces, axis=0)

        qt_all = _dot_nt(wt_ref[C_SBQ:C_QEND, :], h)
        sbqt_ref[...] = qt_all[C_SBQ:C_SBQ + 256]
        bgt_ref[...] = qt_all[C_BG:C_BG + 128]
        nqct_ref[...] = qt_all[C_NQ:C_NQ + 512]
        for c in range(NSA_GROUP):
            nqrt_ref[c * LANES:(c + 1) * LANES, :] = rope_t(qt_all[C_NQ + c * LANES:C_NQ + (c + 1) * LANES])
        sbkvt_ref[0, 0] = sbkv_ref[...].T
        nsat_ref[0, 0] = nsa_ref[...].T
        wint_ref[0] = win_ref[...].T


def _in_proj(x, g_pre, w_t, phi_k, phi_v, tables, sbkv_all, nsa_all, layer, n_p, batch, seq):
    n = x.shape[0]
    n_s = n - n_p
    tm = ROW_TILE
    npt = n_p // tm
    per_seq = seq // tm
    cos_r, sin_r, cos_t, sin_t = tables
    row = lambda w: pl.BlockSpec((tm, w), lambda i: (i, 0))
    fixed = lambda s: pl.BlockSpec(s, lambda i: (0, 0), pipeline_mode=pl.Buffered(1))
    smem = pl.BlockSpec(memory_space=pltpu.SMEM)
    anywhere = pl.BlockSpec(memory_space=pl.ANY)
    prompt_tile = lambda i: jnp.minimum(i, npt - 1)
    sample_row = lambda w: pl.BlockSpec((tm, w), lambda i: (jnp.maximum(i - npt, 0), 0))
    prompt_col = lambda f: pl.BlockSpec((f, tm), lambda i: (0, prompt_tile(i)))
    stacked = pl.BlockSpec((1, 1, 512, tm), lambda i: (layer, prompt_tile(i) // per_seq, 0, prompt_tile(i) % per_seq))
    per_batch = pl.BlockSpec((1, 256, tm), lambda i: (prompt_tile(i) // per_seq, 0, prompt_tile(i) % per_seq))
    all_widths = (512, 256, 256, 256, 512, 256, 512)
    sample_widths = (256, 512, 512, 128)
    prompt_feats = (256, 512, 512, 128)
    out_shape = ([jax.ShapeDtypeStruct((n, w), F32) for w in all_widths]
                 + [jax.ShapeDtypeStruct((n // SEL_BLOCK, 512), F32)]
                 + [jax.ShapeDtypeStruct((n_s, w), F32) for w in sample_widths]
                 + [jax.ShapeDtypeStruct((f, n_p), F32) for f in prompt_feats]
                 + [jax.ShapeDtypeStruct(sbkv_all.shape, F32), jax.ShapeDtypeStruct(nsa_all.shape, F32),
                    jax.ShapeDtypeStruct((batch, 256, seq), F32)])
    out_specs = ([row(w) for w in all_widths] + [pl.BlockSpec((tm // SEL_BLOCK, 512), lambda i: (i, 0))]
                 + [sample_row(w) for w in sample_widths] + [prompt_col(f) for f in prompt_feats]
                 + [stacked, stacked, per_batch])
    n_fixed_outs = len(all_widths) + 1 + len(sample_widths) + len(prompt_feats)
    return pl.pallas_call(
        functools.partial(_in_proj_kernel, n_prompt_tiles=npt),
        grid=(n // tm,),
        in_specs=[smem, smem, row(D_MODEL), fixed((1, D_MODEL)), fixed((IN_WIDTH_P, D_MODEL)), row(LANES), row(LANES),
                  pl.BlockSpec((ROT_HALF, tm), lambda i: (0, i)), pl.BlockSpec((ROT_HALF, tm), lambda i: (0, i)),
                  anywhere, anywhere],
        out_specs=out_specs,
        out_shape=out_shape,
        input_output_aliases={9: n_fixed_outs, 10: n_fixed_outs + 1},
        scratch_shapes=[pltpu.VMEM((tm, LANES), F32), pltpu.VMEM((tm, LANES), F32)],
        compiler_params=pltpu.CompilerParams(dimension_semantics=("arbitrary",), vmem_limit_bytes=VMEM_LIMIT),
        name="in_proj",
    )(phi_k, phi_v, x, g_pre, w_t, cos_r, sin_r, cos_t, sin_t, sbkv_all, nsa_all)


def _sb_prompt_kernel(qt_ref, k_ref, vt_ref, g_ref, o_ref, acc_ref, run_ref):
    tq = qt_ref.shape[1]
    tk = tq
    i = pl.program_id(1)
    n_chains = acc_ref.shape[0]
    tc = tq // n_chains
    n_rows = SB_HEADS * tc
    qt = qt_ref[...] * SCALE
    head = lax.broadcasted_iota(jnp.int32, (SB_WIDTH, tc), 0) // HEAD_DIM
    q4t = [jnp.concatenate([jnp.where(head == h, qt[:, c * tc:(c + 1) * tc], 0.0) for h in range(SB_HEADS)],
                           axis=1).astype(BF16) for c in range(n_chains)]
    u_t = _later_matrix(tk, transposed=True)
    qrel = lax.broadcasted_iota(jnp.int32, (tk, n_rows), 1) % tc
    krow = lax.broadcasted_iota(jnp.int32, (tk, n_rows), 0)
    acc_ref[...] = jnp.zeros_like(acc_ref)
    run_ref[...] = jnp.zeros_like(run_ref)

    def step(j, diagonal):
        start = pl.multiple_of(j * tk, tk)
        k = k_ref[pl.ds(start, tk), :].astype(BF16)
        vt = vt_ref[0, 0, :, pl.ds(start, tk)].astype(BF16)
        chains = range(n_chains)
        zs = [_dot(k, q4t[c]) for c in chains]
        log_keeps = [-_softplus(z) for z in zs]
        if diagonal:
            masks = [krow < qrel + c * tc for c in chains]
            log_keeps = [jnp.where(m, lk, 0.0) for m, lk in zip(masks, log_keeps)]
        boths = [_dot(u_t, jnp.concatenate(_split_bf16(lk), axis=1)) for lk in log_keeps]
        runs = [run_ref[c] for c in chains]
        a_s = [jnp.exp(z + lk + (both[:, :n_rows] + both[:, n_rows:] + run))
               for z, lk, both, run in zip(zs, log_keeps, boths, runs)]
        if diagonal:
            a_s = [jnp.where(m, a, 0.0) for m, a in zip(masks, a_s)]
        pvs = [_dot(vt, a.astype(BF16)) for a in a_s]
        for c in chains:
            acc_ref[c] += pvs[c]
            run_ref[c] = runs[c] + jnp.sum(log_keeps[c], axis=0, keepdims=True)

    step(i, True)

    def body(jj, carry):
        step(i - 1 - jj, False)
        return carry

    lax.fori_loop(0, i, body, 0)
    for c in range(n_chains):
        acc = acc_ref[c]
        out_t = jnp.concatenate([acc[h * HEAD_DIM:(h + 1) * HEAD_DIM, h * tc:(h + 1) * tc] for h in range(SB_HEADS)],
                                axis=0)
        o_ref[c * tc:(c + 1) * tc, :] = out_t.T * _silu(g_ref[c * tc:(c + 1) * tc, :])


def _sb_prompt(sbqt, sbkv, sbkv_all, sbg, layer, batch, seq):
    tq = K_TILE
    n_chains = K_TILE // Q_TILE
    nq = seq // tq
    return pl.pallas_call(
        _sb_prompt_kernel,
        grid=(batch, nq),
        in_specs=[pl.BlockSpec((256, tq), lambda b, i: (0, b * nq + i)),
                  pl.BlockSpec((seq, 256), lambda b, i: (b, 0)),
                  pl.BlockSpec((1, 1, 256, seq), lambda b, i: (layer, b, 1, 0)),
                  pl.BlockSpec((tq, 256), lambda b, i: (b * nq + i, 0))],
        out_specs=pl.BlockSpec((tq, 256), lambda b, i: (b * nq + i, 0)),
        out_shape=jax.ShapeDtypeStruct((batch * seq, 256), F32),
        scratch_shapes=[pltpu.VMEM((n_chains, SB_WIDTH, SB_HEADS * Q_TILE), F32),
                        pltpu.VMEM((n_chains, 1, SB_HEADS * Q_TILE), F32)],
        compiler_params=pltpu.CompilerParams(dimension_semantics=("parallel", "parallel"),
                                             vmem_limit_bytes=VMEM_LIMIT),
        name="sb_prompt",
    )(sbqt, sbkv, sbkv_all, sbg)


def _stack_sb_heads(q):
    head = lax.broadcasted_iota(jnp.int32, q.shape, 1) // HEAD_DIM
    return jnp.concatenate([jnp.where(head == h, q, 0.0) for h in range(SB_HEADS)], axis=0).astype(BF16)


def _unstack_sb_heads(acc, t):
    head = lax.broadcasted_iota(jnp.int32, (t, SB_WIDTH), 1) // HEAD_DIM
    out = jnp.zeros((t, SB_WIDTH), F32)
    for h in range(SB_HEADS):
        out = jnp.where(head == h, acc[h * t:(h + 1) * t], out)
    return out


def _sb_sample_kernel(pt_ref, q_ref, kvn_ref, g_ref, *rest, n_pages, page):
    pages = rest[:n_pages]
    o_ref = rest[n_pages]
    new_ref = rest[n_pages + 1]
    t = q_ref.shape[0]
    rows = SB_HEADS * t
    q4 = _stack_sb_heads(q_ref[...] * SCALE)
    u = _later_matrix(page)
    trow = lax.broadcasted_iota(jnp.int32, (rows, page), 0) % t
    scol = lax.broadcasted_iota(jnp.int32, (rows, page), 1)

    new_ref[...] = jnp.zeros_like(new_ref)
    new_ref[0:t, :] = kvn_ref[...]
    mask = scol < trow
    z_new = _dot_nt(q4, new_ref[:, 0:256].astype(BF16))
    order = list(reversed(range(n_pages)))
    zs = [z_new] + [_dot(q4, pages[p][0, 0, 0:256, :].astype(BF16)) for p in order]
    log_keeps = [jnp.where(mask, -_softplus(z_new), 0.0)] + [-_softplus(z) for z in zs[1:]]
    halves = []
    for lk in log_keeps:
        halves += list(_split_bf16(lk))
    both = _dot(jnp.concatenate(halves, axis=0), u)
    run = jnp.zeros((rows, 1), F32)
    acc = jnp.zeros((rows, SB_WIDTH), F32)
    for idx, (z, lk) in enumerate(zip(zs, log_keeps)):
        later = both[2 * idx * rows:(2 * idx + 1) * rows] + both[(2 * idx + 1) * rows:(2 * idx + 2) * rows] + run
        a = jnp.exp(z + lk + later)
        if idx == 0:
            acc = acc + _dot(jnp.where(mask, a, 0.0).astype(BF16), new_ref[:, 256:512].astype(BF16))
        else:
            acc = acc + _dot_nt(a.astype(BF16), pages[order[idx - 1]][0, 0, 256:512, :].astype(BF16))
        run = run + jnp.sum(lk, axis=1, keepdims=True)
    o_ref[...] = _unstack_sb_heads(acc, t) * _silu(g_ref[...])


def _page_specs(layer, n_pages, page):
    def spec(p):
        return pl.BlockSpec((1, 1, 512, page), lambda b, pt: (layer, pt[b, p], 0, 0))
    return [spec(p) for p in range(n_pages)]


def _sb_sample(page_table, sbq, sbkv, sbg, cache_t, layer, row0, dec_batch, t):
    n_pages = page_table.shape[1]
    page = cache_t.shape[3]
    blk0 = row0 // t
    own = lambda w: pl.BlockSpec((t, w), lambda b, pt: (b, 0))
    joint = lambda w: pl.BlockSpec((t, w), lambda b, pt: (blk0 + b, 0))
    grid_spec = pltpu.PrefetchScalarGridSpec(
        num_scalar_prefetch=1,
        grid=(dec_batch,),
        in_specs=[own(256), joint(512), joint(256)] + _page_specs(layer, n_pages, page),
        out_specs=pl.BlockSpec((t, 256), lambda b, pt: (b, 0)),
        scratch_shapes=[pltpu.VMEM((page, 512), F32)],
    )
    return pl.pallas_call(
        functools.partial(_sb_sample_kernel, n_pages=n_pages, page=page),
        grid_spec=grid_spec,
        out_shape=jax.ShapeDtypeStruct((dec_batch * t, 256), F32),
        compiler_params=pltpu.CompilerParams(dimension_semantics=("parallel",), vmem_limit_bytes=VMEM_LIMIT),
        name="sb_sample",
    )(page_table, sbq, sbkv, sbg, *([cache_t] * n_pages))


def _pool_kernel(u_ref, g_ref, w_ref, sc_ref, *rest, nb, t, pos0, has_prefix):
    if has_prefix:
        pre_ref, o_ref, rows_ref = rest
    else:
        o_ref, rows_ref = rest
    pad = POOL_BUF + 1
    u = u_ref[...].reshape(nb, t, POOL_WIDTH)
    rows_ref[:, 0:pad, :] = jnp.zeros((nb, pad, POOL_WIDTH), F32)
    if has_prefix:
        rows_ref[:, 1:pad, :] = pre_ref[0]
    rows_ref[:, pad:pad + t, :] = u
    back = lambda s: rows_ref[:, pad - s:pad - s + t, :]
    sums = {}
    total = u
    done = 1
    for w in POOL_WINDOWS:
        for s in range(done, w):
            total = total + back(s)
        done = w
        sums[w] = total
    lane = lax.broadcasted_iota(jnp.int32, (nb, t, POOL_WIDTH), 2) // POOL_GDIM
    pos = pos0 + lax.broadcasted_iota(jnp.int32, (nb, t, POOL_WIDTH), 1)
    window = jnp.zeros((nb, t, POOL_WIDTH), jnp.int32)
    picked = jnp.zeros((nb, t, POOL_WIDTH), F32)
    for gi, w in enumerate(POOL_WINDOWS):
        window = jnp.where(lane == gi, w, window)
        picked = jnp.where(lane == gi, sums[w], picked)
    count = jnp.minimum(pos + 1, window).astype(F32)
    d = picked / count - u
    y = _dot(d.reshape(nb * t, POOL_WIDTH).astype(BF16), w_ref[...]) * sc_ref[...]
    o_ref[...] = y * _silu(g_ref[...])


def _pool(pu, pg, w_bd, scale, prefix, layer, row0, nb, t, n_seq, pos0):
    rows = nb * t
    blk0 = row0 // rows
    has_prefix = prefix is not None
    in_specs = [pl.BlockSpec((rows, 256), lambda i: (blk0 + i, 0)),
                pl.BlockSpec((rows, 256), lambda i: (blk0 + i, 0)),
                pl.BlockSpec((256, 256), lambda i: (0, 0)),
                pl.BlockSpec((1, 256), lambda i: (0, 0))]
    args = [pu, pg, w_bd, scale]
    if has_prefix:
        in_specs.append(pl.BlockSpec((1, nb, POOL_BUF, 256), lambda i: (layer, i, 0, 0)))
        args.append(prefix)
    return pl.pallas_call(
        functools.partial(_pool_kernel, nb=nb, t=t, pos0=pos0, has_prefix=has_prefix),
        grid=(n_seq // nb,),
        in_specs=in_specs,
        out_specs=pl.BlockSpec((rows, 256), lambda i: (i, 0)),
        out_shape=jax.ShapeDtypeStruct((n_seq * t, 256), F32),
        scratch_shapes=[pltpu.VMEM((nb, POOL_BUF + 1 + t, 256), F32)],
        compiler_params=pltpu.CompilerParams(dimension_semantics=("parallel",), vmem_limit_bytes=VMEM_LIMIT),
        name="pool_sample" if has_prefix else "pool_prompt",
    )(*args)


def _stack_nsa_heads_t(ref, cols, scale):
    t = cols.stop - cols.start
    low = lax.broadcasted_iota(jnp.int32, (LANES, t), 0) < HEAD_DIM
    g0, g1 = [], []
    for c in range(NSA_GROUP):
        ch = ref[c * LANES:(c + 1) * LANES, cols] * scale
        g0.append(jnp.where(low, ch, 0.0))
        g1.append(jnp.where(low, 0.0, ch))
    return jnp.concatenate(g0 + g1, axis=1).astype(BF16)


def _per_group_lanes(x, t):
    a, b = x[:, :t], x[:, t:]
    return jnp.concatenate([a] * NSA_GROUP + [b] * NSA_GROUP, axis=1)


def _select_blocks_t(imp, qpos, n_sel):
    blk = lax.broadcasted_iota(jnp.int32, imp.shape, 0)
    forced = (blk == qpos // SEL_BLOCK) | (blk == 0)
    future = blk * SEL_BLOCK > qpos
    imp = jnp.where(forced, FORCED_SCORE, jnp.where(future, -1.0, imp))
    rank = jnp.zeros(imp.shape, jnp.int32)
    for i in range(n_sel):
        row = imp[i:i + 1, :]
        ahead = (row > imp) | ((row == imp) & (blk > i))
        rank = rank + ahead.astype(jnp.int32)
    return (rank < min(SEL_TOPN, n_sel)).astype(F32)


def _nsa_prompt_kernel(nqct_ref, nqrt_ref, bgt_ref, ng_ref, cmp_ref, ks_ref, vst_ref, wk_ref, wvt_ref,
                       o_ref, m_ref, l_ref, acc_ref, oc_ref, os_ref, ow_ref):
    tq = nqct_ref.shape[1]
    tk = tq
    n_chains = acc_ref.shape[0]
    tc = tq // n_chains
    chains = range(n_chains)
    n_sel = cmp_ref.shape[0]
    i = pl.program_id(1)
    q0 = i * tq
    n_rows = NSA_HEADS * tc
    pos_in = lax.broadcasted_iota(jnp.int32, (1, n_rows), 1) % tc
    pos2_in = lax.broadcasted_iota(jnp.int32, (1, NSA_KV_HEADS * tc), 1) % tc
    krow = lax.broadcasted_iota(jnp.int32, (tk, n_rows), 0)
    blk = lax.broadcasted_iota(jnp.int32, (n_sel, n_rows), 0)

    qrts, q_augs = [], []
    for c in chains:
        cols = slice(c * tc, (c + 1) * tc)
        qct = _stack_nsa_heads_t(nqct_ref, cols, SCALE)
        qrt = _stack_nsa_heads_t(nqrt_ref, cols, SCALE * LOG2E)
        qpos = q0 + c * tc + pos_in
        s_e = _dot(cmp_ref[:, 0:128].astype(BF16), qct)
        s_o = _dot(cmp_ref[:, 128:256].astype(BF16), qct)
        s_e = jnp.where(blk * SEL_BLOCK + (CMP_BLOCK - 1) <= qpos, s_e, NEG_INF)
        s_o = jnp.where(blk * SEL_BLOCK + (SEL_BLOCK - 1) <= qpos, s_o, NEG_INF)
        m = jnp.maximum(jnp.max(s_e, axis=0, keepdims=True), jnp.max(s_o, axis=0, keepdims=True))
        m = jnp.where(m == NEG_INF, 0.0, m)
        p_e = jnp.exp(s_e - m)
        p_o = jnp.exp(s_o - m)
        denom = jnp.maximum(jnp.sum(p_e, axis=0, keepdims=True) + jnp.sum(p_o, axis=0, keepdims=True), 1e-30)
        p_e = p_e / denom
        p_o = p_o / denom
        oc_ref[c] = (_dot_tn(cmp_ref[:, 256:384].astype(BF16), p_e.astype(BF16))
                     + _dot_tn(cmp_ref[:, 384:512].astype(BF16), p_o.astype(BF16)))
        pair = p_e + p_o
        imp = []
        for g in range(NSA_KV_HEADS):
            acc = pair[:, (g * NSA_GROUP) * tc:(g * NSA_GROUP + 1) * tc]
            for h in range(1, NSA_GROUP):
                acc = acc + pair[:, (g * NSA_GROUP + h) * tc:(g * NSA_GROUP + h + 1) * tc]
            imp.append(acc)
        sel = _select_blocks_t(jnp.concatenate(imp, axis=1), q0 + c * tc + pos2_in, n_sel)
        member_bias = _per_group_lanes((sel - 1.0) * (-MASKED), tc).astype(BF16)
        qrts.append(qrt)
        q_augs.append(jnp.concatenate([qrt, member_bias], axis=0))

    def reset():
        m_ref[...] = jnp.full_like(m_ref, NEG_INF)
        l_ref[...] = jnp.zeros_like(l_ref)
        acc_ref[...] = jnp.zeros_like(acc_ref)

    def online_softmax(scores, vt):
        m_olds = [m_ref[c] for c in chains]
        m_news = [jnp.maximum(m_old, jnp.max(s, axis=0, keepdims=True)) for m_old, s in zip(m_olds, scores)]
        ps = [jnp.exp2(s - m_new) for s, m_new in zip(scores, m_news)]
        pvs = [_dot(vt, p.astype(BF16)) for p in ps]
        for c in chains:
            alpha = jnp.exp2(m_olds[c] - m_news[c])
            l_ref[c] = alpha * l_ref[c] + jnp.sum(ps[c], axis=0, keepdims=True)
            acc_ref[c] = alpha * acc_ref[c] + pvs[c]
            m_ref[c] = m_news[c]

    def finish(dst_ref):
        for c in chains:
            dst_ref[c] = acc_ref[c] / jnp.maximum(l_ref[c], 1e-30)

    reset()

    def slc_step(j, diagonal):
        start = pl.multiple_of(j * tk, tk)
        key_blk = (j * tk + lax.broadcasted_iota(jnp.int32, (tk, n_sel), 0)) // SEL_BLOCK
        onehot = (key_blk == lax.broadcasted_iota(jnp.int32, (tk, n_sel), 1)).astype(BF16)
        k_aug = jnp.concatenate([ks_ref[pl.ds(start, tk), :].astype(BF16), onehot], axis=1)
        scores = [_dot(k_aug, q_aug) for q_aug in q_augs]
        if diagonal:
            scores = [jnp.where(krow <= pos_in + c * tc, s, MASKED) for c, s in zip(chains, scores)]
        online_softmax(scores, vst_ref[0, 0, :, pl.ds(start, tk)].astype(BF16))

    def slc_body(j, carry):
        slc_step(j, False)
        return carry

    lax.fori_loop(0, i, slc_body, 0)
    slc_step(i, True)
    finish(os_ref)

    span = WINDOW + tc

    @pl.when(q0 >= WINDOW)
    def _():
        kc = lax.broadcasted_iota(jnp.int32, (span, tc), 0)
        qr = lax.broadcasted_iota(jnp.int32, (span, tc), 1)
        bias = jnp.where((kc > qr) & (kc <= qr + WINDOW), 0.0, MASKED)
        bias = jnp.concatenate([bias] * NSA_HEADS, axis=1)
        starts = [pl.multiple_of(q0 + c * tc - WINDOW, tc) for c in chains]
        scores = [_dot(wk_ref[pl.ds(start, span), :].astype(BF16), qrt) + bias for start, qrt in zip(starts, qrts)]
        ps = [jnp.exp2(s - jnp.max(s, axis=0, keepdims=True)) for s in scores]
        pvs = [_dot(wvt_ref[0, :, pl.ds(start, span)].astype(BF16), p.astype(BF16)) for start, p in zip(starts, ps)]
        for c in chains:
            ow_ref[c] = pvs[c] / jnp.sum(ps[c], axis=0, keepdims=True)

    @pl.when(q0 < WINDOW)
    def _():
        reset()

        def win_body(j, carry):
            start = pl.multiple_of(j * tk, tk)
            k = wk_ref[pl.ds(start, tk), :].astype(BF16)
            dists = [(q0 + c * tc + pos_in) - (krow + j * tk) for c in chains]
            scores = [jnp.where((dist >= 0) & (dist < WINDOW), _dot(k, qrt), MASKED) for dist, qrt in zip(dists, qrts)]
            online_softmax(scores, wvt_ref[0, :, pl.ds(start, tk)].astype(BF16))
            return carry

        lax.fori_loop(0, i + 1, win_body, 0)
        finish(ow_ref)

    low = lax.broadcasted_iota(jnp.int32, (LANES, tc), 0) < HEAD_DIM
    for c in chains:
        rows = slice(c * tc, (c + 1) * tc)
        gate = 1.0 / (1.0 + jnp.exp(-bgt_ref[:, rows]))
        o_c, o_s, o_w = oc_ref[c], os_ref[c], ow_ref[c]
        for ch in range(NSA_GROUP):
            per_group = []
            for g in range(NSA_KV_HEADS):
                hd = g * NSA_GROUP + ch
                sl = slice(hd * tc, (hd + 1) * tc)
                per_group.append(gate[3 * hd:3 * hd + 1, :] * o_c[:, sl] + gate[3 * hd + 1:3 * hd + 2, :] * o_s[:, sl]
                                 + gate[3 * hd + 2:3 * hd + 3, :] * o_w[:, sl])
            mixed = jnp.where(low, per_group[0], per_group[1])
            cs = slice(ch * LANES, (ch + 1) * LANES)
            o_ref[rows, cs] = mixed.T * _silu(ng_ref[rows, cs])


def _nsa_prompt(nqct, nqrt, bgt, ngate, cmp, nsa_new, nsa_all, win_new, win_t, layer, batch, seq):
    tq = K_TILE
    n_chains = K_TILE // Q_TILE
    nq = seq // tq
    n_sel = seq // SEL_BLOCK
    qcol = lambda f: pl.BlockSpec((f, tq), lambda b, i: (0, b * nq + i))
    qrow = lambda w: pl.BlockSpec((tq, w), lambda b, i: (b * nq + i, 0))
    n_rows = NSA_HEADS * Q_TILE
    stat = pltpu.VMEM((n_chains, 1, n_rows), F32)
    wide = pltpu.VMEM((n_chains, LANES, n_rows), F32)
    return pl.pallas_call(
        _nsa_prompt_kernel,
        grid=(batch, nq),
        in_specs=[qcol(512), qcol(512), qcol(128), qrow(512),
                  pl.BlockSpec((n_sel, 512), lambda b, i: (b, 0)),
                  pl.BlockSpec((seq, LANES), lambda b, i: (b, 2)),
                  pl.BlockSpec((1, 1, LANES, seq), lambda b, i: (layer, b, 3, 0)),
                  pl.BlockSpec((seq, LANES), lambda b, i: (b, 0)),
                  pl.BlockSpec((1, LANES, seq), lambda b, i: (b, 1, 0))],
        out_specs=qrow(512),
        out_shape=jax.ShapeDtypeStruct((batch * seq, 512), F32),
        scratch_shapes=[stat, stat, wide, wide, wide, wide],
        compiler_params=pltpu.CompilerParams(dimension_semantics=("parallel", "parallel"),
                                             vmem_limit_bytes=VMEM_LIMIT),
        name="nsa_prompt",
    )(nqct, nqrt, bgt, ngate, cmp, nsa_new, nsa_all, win_new, win_t)


def _stack_nsa_heads(ref, t):
    low = lax.broadcasted_iota(jnp.int32, (t, LANES), 1) < HEAD_DIM
    g0, g1 = [], []
    for c in range(NSA_GROUP):
        ch = ref[:, c * LANES:(c + 1) * LANES] * SCALE
        g0.append(jnp.where(low, ch, 0.0))
        g1.append(jnp.where(low, 0.0, ch))
    return jnp.concatenate(g0 + g1, axis=0).astype(BF16)


def _per_group_rows(x, t):
    a, b = x[:t], x[t:]
    return jnp.concatenate([a] * NSA_GROUP + [b] * NSA_GROUP, axis=0)


def _select_blocks(imp, qpos, n_sel):
    blk = lax.broadcasted_iota(jnp.int32, imp.shape, 1)
    forced = (blk == qpos // SEL_BLOCK) | (blk == 0)
    future = blk * SEL_BLOCK > qpos
    imp = jnp.where(forced, FORCED_SCORE, jnp.where(future, -1.0, imp))
    rank = jnp.zeros(imp.shape, jnp.int32)
    for i in range(n_sel):
        col = imp[:, i:i + 1]
        ahead = (col > imp) | ((col == imp) & (blk > i))
        rank = rank + ahead.astype(jnp.int32)
    return ((rank < min(SEL_TOPN, n_sel)) & (blk < n_sel)).astype(BF16)


def _expand_matrix(n_rows, n_keys, key0):
    r = lax.broadcasted_iota(jnp.int32, (n_rows, n_keys), 0)
    c = lax.broadcasted_iota(jnp.int32, (n_rows, n_keys), 1)
    return ((key0 + c) // SEL_BLOCK == r).astype(BF16)


def _softmax_parts(parts):
    m = None
    for s, _, _ in parts:
        pm = jnp.max(s, axis=1, keepdims=True)
        m = pm if m is None else jnp.maximum(m, pm)
    m = jnp.where(m == NEG_INF, 0.0, m)
    denom = None
    acc = None
    for s, v, feature_major in parts:
        p = jnp.exp(s - m)
        ps = jnp.sum(p, axis=1, keepdims=True)
        pv = _dot_nt(p.astype(BF16), v) if feature_major else _dot(p.astype(BF16), v)
        denom = ps if denom is None else denom + ps
        acc = pv if acc is None else acc + pv
    return acc / jnp.maximum(denom, 1e-30)


def _nsa_sample_kernel(pt_ref, nqc_ref, nqr_ref, new_ref, winn_ref, bg_ref, ng_ref, swin_ref, phi_ref, seg_ref,
                       *rest, n_pages, page, pos0):
    pages = rest[:n_pages]
    o_ref, ckall_ref, cvall_ref, newpad_ref, winpad_ref = rest[n_pages:]
    t = nqc_ref.shape[0]
    rows = NSA_HEADS * t
    n_sel = (pos0 + t + SEL_BLOCK - 1) // SEL_BLOCK
    qc = _stack_nsa_heads(nqc_ref, t)
    qr = _stack_nsa_heads(nqr_ref, t)
    trow = lax.broadcasted_iota(jnp.int32, (rows, 1), 0) % t
    trow2 = lax.broadcasted_iota(jnp.int32, (NSA_KV_HEADS * t, 1), 0) % t
    qpos, qpos2 = pos0 + trow, pos0 + trow2

    for p in range(n_pages):
        ckall_ref[:, p * page:(p + 1) * page] = (pages[p][0, 0, 0:128, :] * phi_ref[0:1, :]).astype(BF16)
        cvall_ref[:, p * page:(p + 1) * page] = (pages[p][0, 0, 128:256, :] * phi_ref[1:2, :]).astype(BF16)
    kcmp_t = _dot(ckall_ref[...], seg_ref[...]).astype(BF16)
    vcmp_t = _dot(cvall_ref[...], seg_ref[...]).astype(BF16)
    s = _dot(qc, kcmp_t)
    col = lax.broadcasted_iota(jnp.int32, (rows, LANES), 1)
    block_end = jnp.where(col < SEL_BLOCK, col * SEL_BLOCK + (CMP_BLOCK - 1),
                          (col - SEL_BLOCK) * SEL_BLOCK + (SEL_BLOCK - 1))
    s = jnp.where(block_end <= qpos, s, NEG_INF)
    m = jnp.max(s, axis=1, keepdims=True)
    m = jnp.where(m == NEG_INF, 0.0, m)
    p_c = jnp.exp(s - m)
    p_c = p_c / jnp.maximum(jnp.sum(p_c, axis=1, keepdims=True), 1e-30)
    o_c = _dot_nt(p_c.astype(BF16), vcmp_t)
    imp = []
    for g in range(NSA_KV_HEADS):
        acc = p_c[(g * NSA_GROUP) * t:(g * NSA_GROUP + 1) * t]
        for c in range(1, NSA_GROUP):
            acc = acc + p_c[(g * NSA_GROUP + c) * t:(g * NSA_GROUP + c + 1) * t]
        imp.append(acc)
    imp = jnp.concatenate(imp, axis=0)
    imp = imp + pltpu.roll(imp, SEL_BLOCK, 1)
    sel = _select_blocks(imp, qpos2, n_sel)

    newpad_ref[...] = jnp.zeros_like(newpad_ref)
    newpad_ref[0:t, :] = new_ref[...]
    winpad_ref[...] = jnp.zeros_like(winpad_ref)
    winpad_ref[0:t, :] = winn_ref[...]
    scol = lax.broadcasted_iota(jnp.int32, (NSA_KV_HEADS * t, page), 1)

    member = _dot(sel, _expand_matrix(LANES, n_pages * page, 0))
    bias_past = _per_group_rows(jnp.where(member > 0.5, 0.0, NEG_INF), t)
    parts = []
    for p in range(n_pages):
        sc = _dot(qr, pages[p][0, 0, 256:384, :].astype(BF16)) + bias_past[:, p * page:(p + 1) * page]
        parts.append((sc, pages[p][0, 0, 384:512, :].astype(BF16), True))
    member_new = _dot(sel, _expand_matrix(LANES, page, n_pages * page))
    ok_new = (member_new > 0.5) & (scol <= trow2) & (scol < t)
    bias_new = _per_group_rows(jnp.where(ok_new, 0.0, NEG_INF), t)
    parts.append((_dot_nt(qr, newpad_ref[:, 256:384].astype(BF16)) + bias_new,
                  newpad_ref[:, 384:512].astype(BF16), False))
    o_s = _softmax_parts(parts)

    wlen = swin_ref.shape[3]
    rcol = lax.broadcasted_iota(jnp.int32, (NSA_KV_HEADS * t, wlen), 1)
    dist = wlen + trow2 - rcol
    bias_w = _per_group_rows(jnp.where((dist >= 0) & (dist < WINDOW), 0.0, NEG_INF), t)
    dist_n = trow2 - scol
    bias_wn = _per_group_rows(jnp.where((dist_n >= 0) & (dist_n < WINDOW) & (scol < t), 0.0, NEG_INF), t)
    parts = [(_dot(qr, swin_ref[0, 0, 0:128, :].astype(BF16)) + bias_w, swin_ref[0, 0, 128:256, :].astype(BF16), True),
             (_dot_nt(qr, winpad_ref[:, 0:128].astype(BF16)) + bias_wn, winpad_ref[:, 128:256].astype(BF16), False)]
    o_w = _softmax_parts(parts)

    gate = 1.0 / (1.0 + jnp.exp(-bg_ref[...]))
    low = lax.broadcasted_iota(jnp.int32, (t, LANES), 1) < HEAD_DIM
    for c in range(NSA_GROUP):
        per_group = []
        for g in range(NSA_KV_HEADS):
            r0 = (g * NSA_GROUP + c) * t
            cg = (g * NSA_GROUP + c) * 3
            per_group.append(gate[:, cg:cg + 1] * o_c[r0:r0 + t] + gate[:, cg + 1:cg + 2] * o_s[r0:r0 + t]
                             + gate[:, cg + 2:cg + 3] * o_w[r0:r0 + t])
        mixed = jnp.where(low, per_group[0], per_group[1])
        sl = slice(c * LANES, (c + 1) * LANES)
        o_ref[:, sl] = mixed * _silu(ng_ref[:, sl])


def _nsa_sample(page_table, nqc, nqr, nsa_new, win_new, bgate, ngate, state_win_t, phi_rows, seg, cache_t,
                layer, row0, dec_batch, t, pos0):
    n_pages = page_table.shape[1]
    page = cache_t.shape[3]
    blk0 = row0 // t
    own = lambda w: pl.BlockSpec((t, w), lambda b, pt: (b, 0))
    joint = lambda w: pl.BlockSpec((t, w), lambda b, pt: (blk0 + b, 0))
    wlen = state_win_t.shape[3]
    n_keys = n_pages * page
    grid_spec = pltpu.PrefetchScalarGridSpec(
        num_scalar_prefetch=1,
        grid=(dec_batch,),
        in_specs=[own(512), own(512), joint(512), joint(256), own(128), joint(512),
                  pl.BlockSpec((1, 1, 256, wlen), lambda b, pt: (layer, b, 0, 0)),
                  pl.BlockSpec((2, page), lambda b, pt: (0, 0)),
                  pl.BlockSpec((n_keys, LANES), lambda b, pt: (0, 0))]
        + _page_specs(layer, n_pages, page),
        out_specs=pl.BlockSpec((t, 512), lambda b, pt: (b, 0)),
        scratch_shapes=[pltpu.VMEM((LANES, n_keys), BF16), pltpu.VMEM((LANES, n_keys), BF16),
                        pltpu.VMEM((page, 512), F32), pltpu.VMEM((page, 256), F32)],
    )
    return pl.pallas_call(
        functools.partial(_nsa_sample_kernel, n_pages=n_pages, page=page, pos0=pos0),
        grid_spec=grid_spec,
        out_shape=jax.ShapeDtypeStruct((dec_batch * t, 512), F32),
        compiler_params=pltpu.CompilerParams(dimension_semantics=("parallel",), vmem_limit_bytes=VMEM_LIMIT),
        name="nsa_sample",
    )(page_table, nqc, nqr, nsa_new, win_new, bgate, ngate, state_win_t, phi_rows, seg, *([cache_t] * n_pages))


def _out_proj_kernel(oap_ref, obp_ref, ocp_ref, oas_ref, obs_ref, ocs_ref, x_ref, w_ref, g_ref, y_ref, *,
                     n_prompt_tiles):
    prompt = pl.program_id(0) < n_prompt_tiles
    pick = lambda p_ref, s_ref: jnp.where(prompt, p_ref[...], s_ref[...]).astype(BF16)
    out = (_dot(pick(oap_ref, oas_ref), w_ref[0:256, :])
           + _dot(pick(obp_ref, obs_ref), w_ref[256:512, :])
           + _dot(pick(ocp_ref, ocs_ref), w_ref[512:1024, :]))
    ms = jnp.mean(out * out, axis=-1, keepdims=True)
    y_ref[...] = x_ref[...] + out * lax.rsqrt(ms + NORM_EPS) * g_ref[...]


def _out_proj(prompt_parts, sample_parts, x, w_out_p, g_post, n_p):
    n = x.shape[0]
    tm = ROW_TILE
    npt = n_p // tm
    row = lambda w: pl.BlockSpec((tm, w), lambda i: (i, 0))
    prompt_row = lambda w: pl.BlockSpec((tm, w), lambda i: (jnp.minimum(i, npt - 1), 0))
    sample_row = lambda w: pl.BlockSpec((tm, w), lambda i: (jnp.maximum(i - npt, 0), 0))
    fixed = lambda s: pl.BlockSpec(s, lambda i: (0, 0), pipeline_mode=pl.Buffered(1))
    widths = (256, 256, 512)
    return pl.pallas_call(
        functools.partial(_out_proj_kernel, n_prompt_tiles=npt),
        grid=(n // tm,),
        in_specs=[prompt_row(w) for w in widths] + [sample_row(w) for w in widths]
        + [row(D_MODEL), fixed((D_MODEL, D_MODEL)), fixed((1, D_MODEL))],
        out_specs=row(D_MODEL),
        out_shape=jax.ShapeDtypeStruct((n, D_MODEL), F32),
        compiler_params=pltpu.CompilerParams(dimension_semantics=("arbitrary",), vmem_limit_bytes=VMEM_LIMIT),
        name="out_proj",
    )(*prompt_parts, *sample_parts, x, w_out_p, g_post)


def _win_update_kernel(state_ref, new_ref, o_ref, *, nb, t):
    wlen = state_ref.shape[3]
    for bi in range(nb):
        o_ref[0, bi] = pltpu.roll(state_ref[0, bi], wlen - t, 1)
        o_ref[0, bi, :, wlen - t:wlen] = new_ref[0, bi * t:(bi + 1) * t, :].T


def _win_update(state_win_t, win_rows, t):
    depth, bs, width, wlen = state_win_t.shape
    nb = 8
    return pl.pallas_call(
        functools.partial(_win_update_kernel, nb=nb, t=t),
        grid=(depth, bs // nb),
        in_specs=[pl.BlockSpec((1, nb, width, wlen), lambda l, b: (l, b, 0, 0)),
                  pl.BlockSpec((1, nb * t, width), lambda l, b: (l, b, 0))],
        out_specs=pl.BlockSpec((1, nb, width, wlen), lambda l, b: (l, b, 0, 0)),
        out_shape=jax.ShapeDtypeStruct(state_win_t.shape, F32),
        compiler_params=pltpu.CompilerParams(dimension_semantics=("parallel", "parallel"),
                                             vmem_limit_bytes=VMEM_LIMIT),
        name="win_update",
    )(state_win_t, win_rows)


def _to_chunk_layout(w, axis):
    shape = w.shape
    split = shape[:axis] + (NSA_KV_HEADS, NSA_GROUP, HEAD_DIM) + shape[axis + 1:]
    return jnp.swapaxes(w.reshape(split), axis, axis + 1).reshape(shape)


def _rope_tables(pos):
    inv = ROPE_THETA ** (-jnp.arange(ROT_HALF, dtype=F32) * (2.0 / ROT_DIM))
    ang = pos.astype(F32)[:, None] * inv[None, :]
    cos, sin = jnp.cos(ang), jnp.sin(ang)
    n = pos.shape[0]
    cos_h = jnp.concatenate([cos, cos, jnp.ones((n, HEAD_DIM - ROT_DIM), F32)], axis=1)
    sin_h = jnp.concatenate([-sin, sin, jnp.zeros((n, HEAD_DIM - ROT_DIM), F32)], axis=1)
    reps = (1, LANES // HEAD_DIM)
    return jnp.tile(cos_h, reps), jnp.tile(sin_h, reps), cos.T, sin.T


def _segment_matrix(n_keys):
    n = np.arange(n_keys) // CMP_BLOCK
    col = (n % 2) * SEL_BLOCK + n // 2
    return jnp.asarray(col[:, None] == np.arange(LANES)[None, :], BF16)


def kernel(x_prompt, x_sample, cache_sb_kv, cache_nsa_kv, state_win_kv, state_pool, page_table, norm_pre, w_in,
           pool_w, pool_scale, phi_k, phi_v, w_out, norm_post):
    bp, tp, d = x_prompt.shape
    bs, ts, _ = x_sample.shape
    depth = w_in.shape[0]
    n_pool, page = cache_sb_kv.shape[1], cache_sb_kv.shape[2]
    past_len = page_table.shape[1] * page
    wlen = state_win_kv.shape[2]
    n_p, n_s = bp * tp, bs * ts
    assert d == D_MODEL and n_p % ROW_TILE == 0 and n_s % ROW_TILE == 0
    assert tp % K_TILE == 0 and K_TILE % Q_TILE == 0 and tp % SEL_BLOCK == 0 and n_p % n_s == 0
    assert past_len % SEL_BLOCK == 0 and ts < CMP_BLOCK and ts % 8 == 0 and page == LANES
    assert past_len // CMP_BLOCK <= SEL_BLOCK and (wlen >= WINDOW or wlen == past_len)

    o = np.cumsum((0,) + (256,) * 6 + (512,) + (128,) * 6 + (24, 512)).tolist()
    w_rows = jnp.swapaxes(w_in, 1, 2)
    w_t = jnp.concatenate([
        w_rows[:, o[0]:o[1]], _to_chunk_layout(w_rows[:, o[6]:o[7]], 1), w_rows[:, o[13]:o[14]],
        jnp.zeros((depth, IN_WIDTH_P - o[15], D_MODEL), w_in.dtype),
        w_rows[:, o[1]:o[6]], w_rows[:, o[7]:o[13]], _to_chunk_layout(w_rows[:, o[14]:o[15]], 1)],
        axis=1).astype(BF16)
    w_out_p = jnp.concatenate([w_out[:, :512], _to_chunk_layout(w_out[:, 512:], 1)], axis=1).astype(BF16)
    w_pool = jnp.zeros((depth, POOL_WIDTH, POOL_WIDTH), F32)
    for g in range(len(POOL_WINDOWS)):
        sl = slice(g * POOL_GDIM, (g + 1) * POOL_GDIM)
        w_pool = w_pool.at[:, sl, sl].set(pool_w[:, g])
    w_pool = w_pool.astype(BF16)
    reps = page // CMP_BLOCK
    phi_rows = jnp.stack([jnp.tile(phi_k, (1, reps)), jnp.tile(phi_v, (1, reps))], axis=1)
    seg = _segment_matrix(past_len)

    pos = jnp.concatenate([jnp.tile(jnp.arange(tp, dtype=jnp.int32), bp),
                           jnp.tile(past_len + jnp.arange(ts, dtype=jnp.int32), bs)])
    tables = _rope_tables(pos)

    cache_sb_t = jnp.transpose(cache_sb_kv, (0, 1, 3, 4, 5, 2)).reshape(depth, n_pool, 512, page)
    cache_nsa_t = jnp.transpose(cache_nsa_kv, (0, 1, 3, 4, 5, 2)).reshape(depth, n_pool, 512, page)
    state_win_t = jnp.transpose(state_win_kv, (0, 1, 3, 4, 5, 2)).reshape(depth, bs, 256, wlen)

    x = jnp.concatenate([x_prompt.reshape(n_p, d), x_sample.reshape(n_s, d)], axis=0)
    sbkv_all = jnp.zeros((depth, bp, 512, tp), F32)
    nsa_all = jnp.zeros((depth, bp, 512, tp), F32)
    keep_p = min(WINDOW, tp)
    win_p, pool_p, sb_s, nsa_s, win_rows, pool_s = [], [], [], [], [], []
    for layer in range(depth):
        (sbkv, sbg, pu, pg, nsa_new, win_new, ngate, cmp, sbq, nqc, nqr, bgate,
         sbqt, nqct, nqrt, bgt, sbkv_all, nsa_all, win_t) = _in_proj(
            x, norm_pre[layer][None], w_t[layer], phi_k[layer], phi_v[layer], tables, sbkv_all, nsa_all,
            layer, n_p, bp, tp)
        pool_args = (pu, pg, w_pool[layer], pool_scale[layer][None])
        prompt_parts = (
            _sb_prompt(sbqt, sbkv, sbkv_all, sbg, layer, bp, tp),
            _pool(*pool_args, None, layer, 0, 1, tp, bp, 0),
            _nsa_prompt(nqct, nqrt, bgt, ngate, cmp, nsa_new, nsa_all, win_new, win_t, layer, bp, tp))
        sample_parts = (
            _sb_sample(page_table, sbq, sbkv, sbg, cache_sb_t, layer, n_p, bs, ts),
            _pool(*pool_args, state_pool, layer, n_p, bs, ts, bs, past_len),
            _nsa_sample(page_table, nqc, nqr, nsa_new, win_new, bgate, ngate, state_win_t, phi_rows[layer], seg,
                        cache_nsa_t, layer, n_p, bs, ts, past_len))
        x = _out_proj(prompt_parts, sample_parts, x, w_out_p[layer], norm_post[layer][None], n_p)

        win_p.append(win_t[:, :, tp - keep_p:])
        pool_p.append(pu[:n_p].reshape(bp, tp, POOL_WIDTH)[:, tp - POOL_BUF:])
        sb_s.append(sbkv[n_p:].reshape(bs, ts, 2, SB_HEADS, HEAD_DIM))
        nsa_s.append(nsa_new[n_p:].reshape(bs, ts, 4, NSA_KV_HEADS, HEAD_DIM))
        win_rows.append(win_new[n_p:])
        pool_s.append(jnp.concatenate([state_pool[layer], pu[n_p:].reshape(bs, ts, 256)], axis=1)[:, -POOL_BUF:])
    win_s_t = _win_update(state_win_t, jnp.stack(win_rows), ts)

    def token_major(a, n_kv, n_heads):
        lead = a.shape[:-2]
        a = a.reshape(lead + (n_kv, n_heads, HEAD_DIM, a.shape[-1]))
        return jnp.moveaxis(a, -1, len(lead))

    y_prompt = x[:n_p].reshape(bp, tp, d)
    y_sample = x[n_p:].reshape(bs, ts, d)
    return (y_prompt, y_sample,
            token_major(sbkv_all, 2, SB_HEADS), token_major(nsa_all, 4, NSA_KV_HEADS),
            token_major(jnp.stack(win_p), 2, NSA_KV_HEADS), jnp.stack(pool_p),
            jnp.stack(sb_s), jnp.stack(nsa_s), token_major(win_s_t, 2, NSA_KV_HEADS), jnp.stack(pool_s))
```

```python
import functools

import jax
import jax.numpy as jnp
import numpy as np
from jax import lax
from jax.experimental import pallas as pl
from jax.experimental.pallas import tpu as pltpu

F32 = jnp.float32
BF16 = jnp.bfloat16

D_MODEL = 1024
HEAD_DIM = 64
SB_WIDTH = 256
SB_HEADS = 4
POOL_WINDOWS = (2, 4, 8, 16)
POOL_WIDTH = 256
POOL_GDIM = 64
POOL_BUF = 15
NSA_WIDTH = 512
NSA_HEADS = 8
NSA_KV_HEADS = 2
NSA_GROUP = 4
NSA_KV_WIDTH = 128
CMP_BLOCK = 32
SEL_BLOCK = 64
SEL_TOPN = 8
WINDOW = 512
ROPE_THETA = 500000.0
ROT_DIM = 16
ROT_HALF = ROT_DIM // 2
NORM_EPS = 1e-6
FORCED_SCORE = 1e4
SCALE = HEAD_DIM ** -0.5
LOG2E = 1.4426950408889634
NEG_INF = float("-inf")
MASKED = -(2.0 ** 100)

C_SBQ, C_NQ, C_BG, C_QEND = 0, 256, 768, 896
C_SBK, C_SBG, C_PU, C_PG = 896, 1408, 1664, 1920
C_NSA = 2176
C_WIN = 2688
C_NG = 2944
IN_WIDTH_P = 3456
LANES = 128
ROW_TILE = 512
Q_TILE = 128
K_TILE = 256
SAMPLE_GROUP = 2
VMEM_LIMIT = 56 * 1024 * 1024


def _dot(a, b):
    return jnp.dot(a, b, preferred_element_type=F32)


def _dot_nt(a, b):
    return lax.dot_general(a, b, (((1,), (1,)), ((), ())), preferred_element_type=F32)


def _dot_tn(a, b):
    return lax.dot_general(a, b, (((0,), (0,)), ((), ())), preferred_element_type=F32)


def _silu(g):
    return g / (1.0 + jnp.exp(-g))


def _softplus(z):
    return jnp.maximum(z, 0.0) + jnp.log(1.0 + jnp.exp(-jnp.abs(z)))


def _split_bf16(x):
    hi = x.astype(BF16)
    return hi, (x - hi.astype(F32)).astype(BF16)


def _suffix_sum_exclusive(x, u_bf16):
    hi, lo = _split_bf16(x)
    n = x.shape[0]
    both = _dot(jnp.concatenate([hi, lo], axis=0), u_bf16)
    return both[:n] + both[n:]


def _later_matrix(n, transposed=False):
    r = lax.broadcasted_iota(jnp.int32, (n, n), 0)
    c = lax.broadcasted_iota(jnp.int32, (n, n), 1)
    return ((c > r) if transposed else (r > c)).astype(BF16)


def _in_proj_kernel(phik_ref, phiv_ref, x_ref, g_ref, wt_ref, cos_ref, sin_ref, cost_ref, sint_ref,
                    sbkv_prev_ref, nsa_prev_ref,
                    sbkv_ref, sbg_ref, pu_ref, pg_ref, nsa_ref, win_ref, ng_ref, cmp_ref,
                    sbq_ref, nqc_ref, nqr_ref, bg_ref,
                    sbqt_ref, nqct_ref, nqrt_ref, bgt_ref, sbkvt_ref, nsat_ref, wint_ref,
                    ck_ref, cv_ref, *, n_prompt_tiles):
    del sbkv_prev_ref, nsa_prev_ref
    i = pl.program_id(0)
    x = x_ref[...]
    ms = jnp.mean(x * x, axis=-1, keepdims=True)
    h = (x * lax.rsqrt(ms + NORM_EPS) * g_ref[...]).astype(BF16)

    def proj(a, b):
        return _dot_nt(h, wt_ref[a:b, :])

    cos = cos_ref[...]
    sin = sin_ref[...]
    first = (lax.broadcasted_iota(jnp.int32, cos.shape, 1) % HEAD_DIM) < ROT_HALF

    def rope(v):
        swapped = jnp.where(first, pltpu.roll(v, LANES - ROT_HALF, 1), pltpu.roll(v, ROT_HALF, 1))
        return v * cos + swapped * sin

    sbkv_ref[...] = proj(C_SBK, C_SBK + 512)
    gates = proj(C_SBG, C_SBG + 768)
    sbg_ref[...] = gates[:, 0:256]
    pu_ref[...] = gates[:, 256:512]
    pg_ref[...] = gates[:, 512:768]
    ng_ref[...] = proj(C_NG, C_NG + 512)
    nsa = proj(C_NSA, C_NSA + 512)
    ck_ref[...] = nsa[:, 0:128]
    cv_ref[...] = nsa[:, 128:256]
    nsa_ref[:, 0:256] = nsa[:, 0:256]
    nsa_ref[:, 256:384] = rope(nsa[:, 256:384])
    nsa_ref[:, 384:512] = nsa[:, 384:512]
    win = proj(C_WIN, C_WIN + 256)
    win_ref[:, 0:128] = rope(win[:, 0:128])
    win_ref[:, 128:256] = win[:, 128:256]

    nb = cmp_ref.shape[0]
    for part, (src_ref, phi_ref) in enumerate(((ck_ref, phik_ref), (cv_ref, phiv_ref))):
        acc_e = jnp.zeros((nb, LANES), F32)
        acc_o = jnp.zeros((nb, LANES), F32)
        for l in range(CMP_BLOCK):
            acc_e = acc_e + src_ref[pl.ds(l, nb, stride=SEL_BLOCK), :] * phi_ref[l]
            acc_o = acc_o + src_ref[pl.ds(CMP_BLOCK + l, nb, stride=SEL_BLOCK), :] * phi_ref[l]
        cmp_ref[:, part * 256:part * 256 + 128] = acc_e
        cmp_ref[:, part * 256 + 128:part * 256 + 256] = acc_o

    @pl.when(i >= n_prompt_tiles)
    def _():
        q_all = proj(C_SBQ, C_QEND)
        sbq_ref[...] = q_all[:, C_SBQ:C_SBQ + 256]
        bg_ref[...] = q_all[:, C_BG:C_BG + 128]
        nqc_ref[...] = q_all[:, C_NQ:C_NQ + 512]
        for c in range(NSA_GROUP):
            nqr_ref[:, c * LANES:(c + 1) * LANES] = rope(q_all[:, C_NQ + c * LANES:C_NQ + (c + 1) * LANES])

    @pl.when(i < n_prompt_tiles)
    def _():
        cos_t = cost_ref[...]
        sin_t = sint_ref[...]

        def rope_t(v):
            pieces = []
            for base in range(0, LANES, HEAD_DIM):
                x1 = v[base:base + ROT_HALF]
                x2 = v[base + ROT_HALF:base + ROT_DIM]
                pieces += [x1 * cos_t - x2 * sin_t, x2 * cos_t + x1 * sin_t, v[base + ROT_DIM:base + HEAD_DIM]]
            return jnp.concatenate(pieces, axis=0)

        qt_all = _dot_nt(wt_ref[C_SBQ:C_QEND, :], h)
        sbqt_ref[...] = qt_all[C_SBQ:C_SBQ + 256]
        bgt_ref[...] = qt_all[C_BG:C_BG + 128]
        nqct_ref[...] = qt_all[C_NQ:C_NQ + 512]
        for c in range(NSA_GROUP):
            nqrt_ref[c * LANES:(c + 1) * LANES, :] = rope_t(qt_all[C_NQ + c * LANES:C_NQ + (c + 1) * LANES])
        sbkvt_ref[0, 0] = sbkv_ref[...].T
        nsat_ref[0, 0] = nsa_ref[...].T
        wint_ref[0] = win_ref[...].T


def _in_proj(x, g_pre, w_t, phi_k, phi_v, tables, sbkv_all, nsa_all, layer, n_p, batch, seq):
    n = x.shape[0]
    n_s = n - n_p
    tm = ROW_TILE
    npt = n_p // tm
    per_seq = seq // tm
    cos_r, sin_r, cos_t, sin_t = tables
    row = lambda w: pl.BlockSpec((tm, w), lambda i: (i, 0))
    fixed = lambda s: pl.BlockSpec(s, lambda i: (0, 0), pipeline_mode=pl.Buffered(1))
    smem = pl.BlockSpec(memory_space=pltpu.SMEM)
    anywhere = pl.BlockSpec(memory_space=pl.ANY)
    prompt_tile = lambda i: jnp.minimum(i, npt - 1)
    sample_row = lambda w: pl.BlockSpec((tm, w), lambda i: (jnp.maximum(i - npt, 0), 0))
    prompt_col = lambda f: pl.BlockSpec((f, tm), lambda i: (0, prompt_tile(i)))
    stacked = pl.BlockSpec((1, 1, 512, tm), lambda i: (layer, prompt_tile(i) // per_seq, 0, prompt_tile(i) % per_seq))
    per_batch = pl.BlockSpec((1, 256, tm), lambda i: (prompt_tile(i) // per_seq, 0, prompt_tile(i) % per_seq))
    all_widths = (512, 256, 256, 256, 512, 256, 512)
    sample_widths = (256, 512, 512, 128)
    prompt_feats = (256, 512, 512, 128)
    out_shape = ([jax.ShapeDtypeStruct((n, w), F32) for w in all_widths]
                 + [jax.ShapeDtypeStruct((n // SEL_BLOCK, 512), F32)]
                 + [jax.ShapeDtypeStruct((n_s, w), F32) for w in sample_widths]
                 + [jax.ShapeDtypeStruct((f, n_p), F32) for f in prompt_feats]
                 + [jax.ShapeDtypeStruct(sbkv_all.shape, F32), jax.ShapeDtypeStruct(nsa_all.shape, F32),
                    jax.ShapeDtypeStruct((batch, 256, seq), F32)])
    out_specs = ([row(w) for w in all_widths] + [pl.BlockSpec((tm // SEL_BLOCK, 512), lambda i: (i, 0))]
                 + [sample_row(w) for w in sample_widths] + [prompt_col(f) for f in prompt_feats]
                 + [stacked, stacked, per_batch])
    n_fixed_outs = len(all_widths) + 1 + len(sample_widths) + len(prompt_feats)
    return pl.pallas_call(
        functools.partial(_in_proj_kernel, n_prompt_tiles=npt),
        grid=(n // tm,),
        in_specs=[smem, smem, row(D_MODEL), fixed((1, D_MODEL)), fixed((IN_WIDTH_P, D_MODEL)), row(LANES), row(LANES),
                  pl.BlockSpec((ROT_HALF, tm), lambda i: (0, i)), pl.BlockSpec((ROT_HALF, tm), lambda i: (0, i)),
                  anywhere, anywhere],
        out_specs=out_specs,
        out_shape=out_shape,
        input_output_aliases={9: n_fixed_outs, 10: n_fixed_outs + 1},
        scratch_shapes=[pltpu.VMEM((tm, LANES), F32), pltpu.VMEM((tm, LANES), F32)],
        compiler_params=pltpu.CompilerParams(dimension_semantics=("arbitrary",), vmem_limit_bytes=VMEM_LIMIT),
        name="in_proj",
    )(phi_k, phi_v, x, g_pre, w_t, cos_r, sin_r, cos_t, sin_t, sbkv_all, nsa_all)


def _sb_prompt_kernel(qt_ref, k_ref, vt_ref, g_ref, o_ref, acc_ref, run_ref):
    tq = qt_ref.shape[1]
    tk = tq
    i = pl.program_id(1)
    n_chains = acc_ref.shape[0]
    tc = tq // n_chains
    n_rows = SB_HEADS * tc
    qt = qt_ref[...] * SCALE
    head = lax.broadcasted_iota(jnp.int32, (SB_WIDTH, tc), 0) // HEAD_DIM
    q4t = [jnp.concatenate([jnp.where(head == h, qt[:, c * tc:(c + 1) * tc], 0.0) for h in range(SB_HEADS)],
                           axis=1).astype(BF16) for c in range(n_chains)]
    u_t = _later_matrix(tk, transposed=True)
    qrel = lax.broadcasted_iota(jnp.int32, (tk, n_rows), 1) % tc
    krow = lax.broadcasted_iota(jnp.int32, (tk, n_rows), 0)
    acc_ref[...] = jnp.zeros_like(acc_ref)
    run_ref[...] = jnp.zeros_like(run_ref)

    def step(j, diagonal):
        start = pl.multiple_of(j * tk, tk)
        k = k_ref[pl.ds(start, tk), :].astype(BF16)
        vt = vt_ref[0, 0, :, pl.ds(start, tk)].astype(BF16)
        chains = range(n_chains)
        zs = [_dot(k, q4t[c]) for c in chains]
        log_keeps = [-_softplus(z) for z in zs]
        if diagonal:
            masks = [krow < qrel + c * tc for c in chains]
            log_keeps = [jnp.where(m, lk, 0.0) for m, lk in zip(masks, log_keeps)]
        boths = [_dot(u_t, jnp.concatenate(_split_bf16(lk), axis=1)) for lk in log_keeps]
        runs = [run_ref[c] for c in chains]
        a_s = [jnp.exp(z + lk + (both[:, :n_rows] + both[:, n_rows:] + run))
               for z, lk, both, run in zip(zs, log_keeps, boths, runs)]
        if diagonal:
            a_s = [jnp.where(m, a, 0.0) for m, a in zip(masks, a_s)]
        pvs = [_dot(vt, a.astype(BF16)) for a in a_s]
        for c in chains:
            acc_ref[c] += pvs[c]
            run_ref[c] = runs[c] + jnp.sum(log_keeps[c], axis=0, keepdims=True)

    step(i, True)

    def body(jj, carry):
        step(i - 1 - jj, False)
        return carry

    lax.fori_loop(0, i, body, 0)
    for c in range(n_chains):
        acc = acc_ref[c]
        out_t = jnp.concatenate([acc[h * HEAD_DIM:(h + 1) * HEAD_DIM, h * tc:(h + 1) * tc] for h in range(SB_HEADS)],
                                axis=0)
        o_ref[c * tc:(c + 1) * tc, :] = out_t.T * _silu(g_ref[c * tc:(c + 1) * tc, :])


def _sb_prompt(sbqt, sbkv, sbkv_all, sbg, layer, batch, seq):
    tq = K_TILE
    n_chains = K_TILE // Q_TILE
    nq = seq // tq
    return pl.pallas_call(
        _sb_prompt_kernel,
        grid=(batch, nq),
        in_specs=[pl.BlockSpec((256, tq), lambda b, i: (0, b * nq + i)),
                  pl.BlockSpec((seq, 256), lambda b, i: (b, 0)),
                  pl.BlockSpec((1, 1, 256, seq), lambda b, i: (layer, b, 1, 0)),
                  pl.BlockSpec((tq, 256), lambda b, i: (b * nq + i, 0))],
        out_specs=pl.BlockSpec((tq, 256), lambda b, i: (b * nq + i, 0)),
        out_shape=jax.ShapeDtypeStruct((batch * seq, 256), F32),
        scratch_shapes=[pltpu.VMEM((n_chains, SB_WIDTH, SB_HEADS * Q_TILE), F32),
                        pltpu.VMEM((n_chains, 1, SB_HEADS * Q_TILE), F32)],
        compiler_params=pltpu.CompilerParams(dimension_semantics=("parallel", "parallel"),
                                             vmem_limit_bytes=VMEM_LIMIT),
        name="sb_prompt",
    )(sbqt, sbkv, sbkv_all, sbg)


def _stack_sb_heads(q):
    head = lax.broadcasted_iota(jnp.int32, q.shape, 1) // HEAD_DIM
    return jnp.concatenate([jnp.where(head == h, q, 0.0) for h in range(SB_HEADS)], axis=0).astype(BF16)


def _unstack_sb_heads(acc, t):
    head = lax.broadcasted_iota(jnp.int32, (t, SB_WIDTH), 1) // HEAD_DIM
    out = jnp.zeros((t, SB_WIDTH), F32)
    for h in range(SB_HEADS):
        out = jnp.where(head == h, acc[h * t:(h + 1) * t], out)
    return out


def _side_by_side(make_body, n):
    live = [make_body(k) for k in range(n)]
    while live:
        still = []
        for gen in live:
            try:
                next(gen)
                still.append(gen)
            except StopIteration:
                pass
        live = still


def _sb_sample_kernel(pt_ref, q_ref, kvn_ref, g_ref, *rest, n_pages, page, group):
    all_pages = rest[:n_pages * group]
    o_ref = rest[n_pages * group]
    new_ref = rest[n_pages * group + 1]
    t = q_ref.shape[0] // group
    rows = SB_HEADS * t
    u = _later_matrix(page)
    trow = lax.broadcasted_iota(jnp.int32, (rows, page), 0) % t
    scol = lax.broadcasted_iota(jnp.int32, (rows, page), 1)
    mask = scol < trow
    order = list(reversed(range(n_pages)))

    def body(k):
        rs = slice(k * t, (k + 1) * t)
        pages = all_pages[k * n_pages:(k + 1) * n_pages]
        q4 = _stack_sb_heads(q_ref[rs, :] * SCALE)
        new_ref[k] = jnp.zeros(new_ref.shape[1:], F32)
        new_ref[k, 0:t, :] = kvn_ref[rs, :]
        z_new = _dot_nt(q4, new_ref[k, :, 0:256].astype(BF16))
        yield
        zs = [z_new] + [_dot(q4, pages[p][0, 0, 0:256, :].astype(BF16)) for p in order]
        yield
        log_keeps = [jnp.where(mask, -_softplus(z_new), 0.0)] + [-_softplus(z) for z in zs[1:]]
        halves = []
        for lk in log_keeps:
            halves += list(_split_bf16(lk))
        yield
        both = _dot(jnp.concatenate(halves, axis=0), u)
        yield
        run = jnp.zeros((rows, 1), F32)
        acc = jnp.zeros((rows, SB_WIDTH), F32)
        for idx, (z, lk) in enumerate(zip(zs, log_keeps)):
            later = both[2 * idx * rows:(2 * idx + 1) * rows] + both[(2 * idx + 1) * rows:(2 * idx + 2) * rows] + run
            a = jnp.exp(z + lk + later)
            if idx == 0:
                acc = acc + _dot(jnp.where(mask, a, 0.0).astype(BF16), new_ref[k, :, 256:512].astype(BF16))
            else:
                acc = acc + _dot_nt(a.astype(BF16), pages[order[idx - 1]][0, 0, 256:512, :].astype(BF16))
            run = run + jnp.sum(lk, axis=1, keepdims=True)
        yield
        o_ref[rs, :] = _unstack_sb_heads(acc, t) * _silu(g_ref[rs, :])

    _side_by_side(body, group)


def _page_specs(layer, n_pages, page, group):
    def spec(k, p):
        return pl.BlockSpec((1, 1, 512, page), lambda b, pt: (layer, pt[b * group + k, p], 0, 0))
    return [spec(k, p) for k in range(group) for p in range(n_pages)]


def _sb_sample(page_table, sbq, sbkv, sbg, cache_t, layer, row0, dec_batch, t):
    n_pages = page_table.shape[1]
    page = cache_t.shape[3]
    group = SAMPLE_GROUP
    gt = group * t
    blk0 = row0 // gt
    own = lambda w: pl.BlockSpec((gt, w), lambda b, pt: (b, 0))
    joint = lambda w: pl.BlockSpec((gt, w), lambda b, pt: (blk0 + b, 0))
    grid_spec = pltpu.PrefetchScalarGridSpec(
        num_scalar_prefetch=1,
        grid=(dec_batch // group,),
        in_specs=[own(256), joint(512), joint(256)] + _page_specs(layer, n_pages, page, group),
        out_specs=pl.BlockSpec((gt, 256), lambda b, pt: (b, 0)),
        scratch_shapes=[pltpu.VMEM((group, page, 512), F32)],
    )
    return pl.pallas_call(
        functools.partial(_sb_sample_kernel, n_pages=n_pages, page=page, group=group),
        grid_spec=grid_spec,
        out_shape=jax.ShapeDtypeStruct((dec_batch * t, 256), F32),
        compiler_params=pltpu.CompilerParams(dimension_semantics=("parallel",), vmem_limit_bytes=VMEM_LIMIT),
        name="sb_sample",
    )(page_table, sbq, sbkv, sbg, *([cache_t] * (n_pages * group)))


def _pool_kernel(u_ref, g_ref, w_ref, sc_ref, *rest, nb, t, pos0, has_prefix):
    if has_prefix:
        pre_ref, o_ref, rows_ref = rest
    else:
        o_ref, rows_ref = rest
    pad = POOL_BUF + 1
    u = u_ref[...].reshape(nb, t, POOL_WIDTH)
    rows_ref[:, 0:pad, :] = jnp.zeros((nb, pad, POOL_WIDTH), F32)
    if has_prefix:
        rows_ref[:, 1:pad, :] = pre_ref[0]
    rows_ref[:, pad:pad + t, :] = u
    back = lambda s: rows_ref[:, pad - s:pad - s + t, :]
    sums = {}
    total = u
    done = 1
    for w in POOL_WINDOWS:
        for s in range(done, w):
            total = total + back(s)
        done = w
        sums[w] = total
    lane = lax.broadcasted_iota(jnp.int32, (nb, t, POOL_WIDTH), 2) // POOL_GDIM
    pos = pos0 + lax.broadcasted_iota(jnp.int32, (nb, t, POOL_WIDTH), 1)
    window = jnp.zeros((nb, t, POOL_WIDTH), jnp.int32)
    picked = jnp.zeros((nb, t, POOL_WIDTH), F32)
    for gi, w in enumerate(POOL_WINDOWS):
        window = jnp.where(lane == gi, w, window)
        picked = jnp.where(lane == gi, sums[w], picked)
    count = jnp.minimum(pos + 1, window).astype(F32)
    d = picked / count - u
    y = _dot(d.reshape(nb * t, POOL_WIDTH).astype(BF16), w_ref[...]) * sc_ref[...]
    o_ref[...] = y * _silu(g_ref[...])


def _pool(pu, pg, w_bd, scale, prefix, layer, row0, nb, t, n_seq, pos0):
    rows = nb * t
    blk0 = row0 // rows
    has_prefix = prefix is not None
    in_specs = [pl.BlockSpec((rows, 256), lambda i: (blk0 + i, 0)),
                pl.BlockSpec((rows, 256), lambda i: (blk0 + i, 0)),
                pl.BlockSpec((256, 256), lambda i: (0, 0)),
                pl.BlockSpec((1, 256), lambda i: (0, 0))]
    args = [pu, pg, w_bd, scale]
    if has_prefix:
        in_specs.append(pl.BlockSpec((1, nb, POOL_BUF, 256), lambda i: (layer, i, 0, 0)))
        args.append(prefix)
    return pl.pallas_call(
        functools.partial(_pool_kernel, nb=nb, t=t, pos0=pos0, has_prefix=has_prefix),
        grid=(n_seq // nb,),
        in_specs=in_specs,
        out_specs=pl.BlockSpec((rows, 256), lambda i: (i, 0)),
        out_shape=jax.ShapeDtypeStruct((n_seq * t, 256), F32),
        scratch_shapes=[pltpu.VMEM((nb, POOL_BUF + 1 + t, 256), F32)],
        compiler_params=pltpu.CompilerParams(dimension_semantics=("parallel",), vmem_limit_bytes=VMEM_LIMIT),
        name="pool_sample" if has_prefix else "pool_prompt",
    )(*args)


def _stack_nsa_heads_t(ref, cols, scale):
    t = cols.stop - cols.start
    low = lax.broadcasted_iota(jnp.int32, (LANES, t), 0) < HEAD_DIM
    g0, g1 = [], []
    for c in range(NSA_GROUP):
        ch = ref[c * LANES:(c + 1) * LANES, cols] * scale
        g0.append(jnp.where(low, ch, 0.0))
        g1.append(jnp.where(low, 0.0, ch))
    return jnp.concatenate(g0 + g1, axis=1).astype(BF16)


def _per_group_lanes(x, t):
    a, b = x[:, :t], x[:, t:]
    return jnp.concatenate([a] * NSA_GROUP + [b] * NSA_GROUP, axis=1)


def _select_blocks_t(imp, qpos, n_sel):
    blk = lax.broadcasted_iota(jnp.int32, imp.shape, 0)
    forced = (blk == qpos // SEL_BLOCK) | (blk == 0)
    future = blk * SEL_BLOCK > qpos
    imp = jnp.where(forced, FORCED_SCORE, jnp.where(future, -1.0, imp))
    rank = jnp.zeros(imp.shape, jnp.int32)
    for i in range(n_sel):
        row = imp[i:i + 1, :]
        ahead = (row > imp) | ((row == imp) & (blk > i))
        rank = rank + ahead.astype(jnp.int32)
    return (rank < min(SEL_TOPN, n_sel)).astype(F32)


def _nsa_prompt_kernel(nqct_ref, nqrt_ref, bgt_ref, ng_ref, cmp_ref, ks_ref, vst_ref, wk_ref, wvt_ref,
                       o_ref, m_ref, l_ref, acc_ref, oc_ref, os_ref, ow_ref):
    tq = nqct_ref.shape[1]
    tk = tq
    n_chains = acc_ref.shape[0]
    tc = tq // n_chains
    chains = range(n_chains)
    n_sel = cmp_ref.shape[0]
    i = pl.program_id(1)
    q0 = i * tq
    n_rows = NSA_HEADS * tc
    pos_in = lax.broadcasted_iota(jnp.int32, (1, n_rows), 1) % tc
    pos2_in = lax.broadcasted_iota(jnp.int32, (1, NSA_KV_HEADS * tc), 1) % tc
    krow = lax.broadcasted_iota(jnp.int32, (tk, n_rows), 0)
    blk = lax.broadcasted_iota(jnp.int32, (n_sel, n_rows), 0)

    qrts, q_augs = [None] * n_chains, [None] * n_chains

    def select_body(c):
        cols = slice(c * tc, (c + 1) * tc)
        qct = _stack_nsa_heads_t(nqct_ref, cols, SCALE)
        qrt = _stack_nsa_heads_t(nqrt_ref, cols, SCALE * LOG2E)
        qpos = q0 + c * tc + pos_in
        yield
        s_e = _dot(cmp_ref[:, 0:128].astype(BF16), qct)
        s_o = _dot(cmp_ref[:, 128:256].astype(BF16), qct)
        yield
        s_e = jnp.where(blk * SEL_BLOCK + (CMP_BLOCK - 1) <= qpos, s_e, NEG_INF)
        s_o = jnp.where(blk * SEL_BLOCK + (SEL_BLOCK - 1) <= qpos, s_o, NEG_INF)
        m = jnp.maximum(jnp.max(s_e, axis=0, keepdims=True), jnp.max(s_o, axis=0, keepdims=True))
        m = jnp.where(m == NEG_INF, 0.0, m)
        p_e = jnp.exp(s_e - m)
        p_o = jnp.exp(s_o - m)
        denom = jnp.maximum(jnp.sum(p_e, axis=0, keepdims=True) + jnp.sum(p_o, axis=0, keepdims=True), 1e-30)
        p_e = p_e / denom
        p_o = p_o / denom
        yield
        oc_ref[c] = (_dot_tn(cmp_ref[:, 256:384].astype(BF16), p_e.astype(BF16))
                     + _dot_tn(cmp_ref[:, 384:512].astype(BF16), p_o.astype(BF16)))
        pair = p_e + p_o
        imp = []
        for g in range(NSA_KV_HEADS):
            acc = pair[:, (g * NSA_GROUP) * tc:(g * NSA_GROUP + 1) * tc]
            for h in range(1, NSA_GROUP):
                acc = acc + pair[:, (g * NSA_GROUP + h) * tc:(g * NSA_GROUP + h + 1) * tc]
            imp.append(acc)
        yield
        sel = _select_blocks_t(jnp.concatenate(imp, axis=1), q0 + c * tc + pos2_in, n_sel)
        member_bias = _per_group_lanes((sel - 1.0) * (-MASKED), tc).astype(BF16)
        qrts[c] = qrt
        q_augs[c] = jnp.concatenate([qrt, member_bias], axis=0)

    _side_by_side(select_body, n_chains)

    def reset():
        m_ref[...] = jnp.full_like(m_ref, NEG_INF)
        l_ref[...] = jnp.zeros_like(l_ref)
        acc_ref[...] = jnp.zeros_like(acc_ref)

    def online_softmax(scores, vt):
        m_olds = [m_ref[c] for c in chains]
        m_news = [jnp.maximum(m_old, jnp.max(s, axis=0, keepdims=True)) for m_old, s in zip(m_olds, scores)]
        ps = [jnp.exp2(s - m_new) for s, m_new in zip(scores, m_news)]
        pvs = [_dot(vt, p.astype(BF16)) for p in ps]
        for c in chains:
            alpha = jnp.exp2(m_olds[c] - m_news[c])
            l_ref[c] = alpha * l_ref[c] + jnp.sum(ps[c], axis=0, keepdims=True)
            acc_ref[c] = alpha * acc_ref[c] + pvs[c]
            m_ref[c] = m_news[c]

    def finish(dst_ref):
        for c in chains:
            dst_ref[c] = acc_ref[c] / jnp.maximum(l_ref[c], 1e-30)

    reset()

    def slc_step(j, diagonal):
        start = pl.multiple_of(j * tk, tk)
        key_blk = (j * tk + lax.broadcasted_iota(jnp.int32, (tk, n_sel), 0)) // SEL_BLOCK
        onehot = (key_blk == lax.broadcasted_iota(jnp.int32, (tk, n_sel), 1)).astype(BF16)
        k_aug = jnp.concatenate([ks_ref[pl.ds(start, tk), :].astype(BF16), onehot], axis=1)
        scores = [_dot(k_aug, q_aug) for q_aug in q_augs]
        if diagonal:
            scores = [jnp.where(krow <= pos_in + c * tc, s, MASKED) for c, s in zip(chains, scores)]
        online_softmax(scores, vst_ref[0, 0, :, pl.ds(start, tk)].astype(BF16))

    def slc_body(j, carry):
        slc_step(j, False)
        return carry

    lax.fori_loop(0, i, slc_body, 0)
    slc_step(i, True)
    finish(os_ref)

    span = WINDOW + tc

    @pl.when(q0 >= WINDOW)
    def _():
        kc = lax.broadcasted_iota(jnp.int32, (span, tc), 0)
        qr = lax.broadcasted_iota(jnp.int32, (span, tc), 1)
        bias = jnp.where((kc > qr) & (kc <= qr + WINDOW), 0.0, MASKED)
        bias = jnp.concatenate([bias] * NSA_HEADS, axis=1)
        starts = [pl.multiple_of(q0 + c * tc - WINDOW, tc) for c in chains]
        scores = [_dot(wk_ref[pl.ds(start, span), :].astype(BF16), qrt) + bias for start, qrt in zip(starts, qrts)]
        ps = [jnp.exp2(s - jnp.max(s, axis=0, keepdims=True)) for s in scores]
        pvs = [_dot(wvt_ref[0, :, pl.ds(start, span)].astype(BF16), p.astype(BF16)) for start, p in zip(starts, ps)]
        for c in chains:
            ow_ref[c] = pvs[c] / jnp.sum(ps[c], axis=0, keepdims=True)

    @pl.when(q0 < WINDOW)
    def _():
        reset()

        def win_body(j, carry):
            start = pl.multiple_of(j * tk, tk)
            k = wk_ref[pl.ds(start, tk), :].astype(BF16)
            dists = [(q0 + c * tc + pos_in) - (krow + j * tk) for c in chains]
            scores = [jnp.where((dist >= 0) & (dist < WINDOW), _dot(k, qrt), MASKED) for dist, qrt in zip(dists, qrts)]
            online_softmax(scores, wvt_ref[0, :, pl.ds(start, tk)].astype(BF16))
            return carry

        lax.fori_loop(0, i + 1, win_body, 0)
        finish(ow_ref)

    low = lax.broadcasted_iota(jnp.int32, (LANES, tc), 0) < HEAD_DIM
    for c in chains:
        rows = slice(c * tc, (c + 1) * tc)
        gate = 1.0 / (1.0 + jnp.exp(-bgt_ref[:, rows]))
        o_c, o_s, o_w = oc_ref[c], os_ref[c], ow_ref[c]
        for ch in range(NSA_GROUP):
            per_group = []
            for g in range(NSA_KV_HEADS):
                hd = g * NSA_GROUP + ch
                sl = slice(hd * tc, (hd + 1) * tc)
                per_group.append(gate[3 * hd:3 * hd + 1, :] * o_c[:, sl] + gate[3 * hd + 1:3 * hd + 2, :] * o_s[:, sl]
                                 + gate[3 * hd + 2:3 * hd + 3, :] * o_w[:, sl])
            mixed = jnp.where(low, per_group[0], per_group[1])
            cs = slice(ch * LANES, (ch + 1) * LANES)
            o_ref[rows, cs] = mixed.T * _silu(ng_ref[rows, cs])


def _nsa_prompt(nqct, nqrt, bgt, ngate, cmp, nsa_new, nsa_all, win_new, win_t, layer, batch, seq):
    tq = K_TILE
    n_chains = K_TILE // Q_TILE
    nq = seq // tq
    n_sel = seq // SEL_BLOCK
    qcol = lambda f: pl.BlockSpec((f, tq), lambda b, i: (0, b * nq + i))
    qrow = lambda w: pl.BlockSpec((tq, w), lambda b, i: (b * nq + i, 0))
    n_rows = NSA_HEADS * Q_TILE
    stat = pltpu.VMEM((n_chains, 1, n_rows), F32)
    wide = pltpu.VMEM((n_chains, LANES, n_rows), F32)
    return pl.pallas_call(
        _nsa_prompt_kernel,
        grid=(batch, nq),
        in_specs=[qcol(512), qcol(512), qcol(128), qrow(512),
                  pl.BlockSpec((n_sel, 512), lambda b, i: (b, 0)),
                  pl.BlockSpec((seq, LANES), lambda b, i: (b, 2)),
                  pl.BlockSpec((1, 1, LANES, seq), lambda b, i: (layer, b, 3, 0)),
                  pl.BlockSpec((seq, LANES), lambda b, i: (b, 0)),
                  pl.BlockSpec((1, LANES, seq), lambda b, i: (b, 1, 0))],
        out_specs=qrow(512),
        out_shape=jax.ShapeDtypeStruct((batch * seq, 512), F32),
        scratch_shapes=[stat, stat, wide, wide, wide, wide],
        compiler_params=pltpu.CompilerParams(dimension_semantics=("parallel", "parallel"),
                                             vmem_limit_bytes=VMEM_LIMIT),
        name="nsa_prompt",
    )(nqct, nqrt, bgt, ngate, cmp, nsa_new, nsa_all, win_new, win_t)


def _stack_nsa_heads(ref, rows):
    low = lax.broadcasted_iota(jnp.int32, (rows.stop - rows.start, LANES), 1) < HEAD_DIM
    g0, g1 = [], []
    for c in range(NSA_GROUP):
        ch = ref[rows, c * LANES:(c + 1) * LANES] * SCALE
        g0.append(jnp.where(low, ch, 0.0))
        g1.append(jnp.where(low, 0.0, ch))
    return jnp.concatenate(g0 + g1, axis=0).astype(BF16)


def _per_group_rows(x, t):
    a, b = x[:t], x[t:]
    return jnp.concatenate([a] * NSA_GROUP + [b] * NSA_GROUP, axis=0)


def _select_blocks(imp, qpos, n_sel):
    blk = lax.broadcasted_iota(jnp.int32, imp.shape, 1)
    forced = (blk == qpos // SEL_BLOCK) | (blk == 0)
    future = blk * SEL_BLOCK > qpos
    imp = jnp.where(forced, FORCED_SCORE, jnp.where(future, -1.0, imp))
    rank = jnp.zeros(imp.shape, jnp.int32)
    for i in range(n_sel):
        col = imp[:, i:i + 1]
        ahead = (col > imp) | ((col == imp) & (blk > i))
        rank = rank + ahead.astype(jnp.int32)
    return ((rank < min(SEL_TOPN, n_sel)) & (blk < n_sel)).astype(BF16)


def _expand_matrix(n_rows, n_keys, key0):
    r = lax.broadcasted_iota(jnp.int32, (n_rows, n_keys), 0)
    c = lax.broadcasted_iota(jnp.int32, (n_rows, n_keys), 1)
    return ((key0 + c) // SEL_BLOCK == r).astype(BF16)


def _softmax_parts(parts):
    m = None
    for s, _, _ in parts:
        pm = jnp.max(s, axis=1, keepdims=True)
        m = pm if m is None else jnp.maximum(m, pm)
    m = jnp.where(m == NEG_INF, 0.0, m)
    denom = None
    acc = None
    for s, v, feature_major in parts:
        p = jnp.exp(s - m)
        ps = jnp.sum(p, axis=1, keepdims=True)
        pv = _dot_nt(p.astype(BF16), v) if feature_major else _dot(p.astype(BF16), v)
        denom = ps if denom is None else denom + ps
        acc = pv if acc is None else acc + pv
    return acc / jnp.maximum(denom, 1e-30)


def _nsa_sample_kernel(pt_ref, nqc_ref, nqr_ref, new_ref, winn_ref, bg_ref, ng_ref, swin_ref, phi_ref, seg_ref,
                       *rest, n_pages, page, pos0, group):
    all_pages = rest[:n_pages * group]
    o_ref, ckall_ref, cvall_ref, newpad_ref, winpad_ref = rest[n_pages * group:]
    t = nqc_ref.shape[0] // group
    rows = NSA_HEADS * t
    n_sel = (pos0 + t + SEL_BLOCK - 1) // SEL_BLOCK
    wlen = swin_ref.shape[3]
    trow = lax.broadcasted_iota(jnp.int32, (rows, 1), 0) % t
    trow2 = lax.broadcasted_iota(jnp.int32, (NSA_KV_HEADS * t, 1), 0) % t
    qpos, qpos2 = pos0 + trow, pos0 + trow2
    col = lax.broadcasted_iota(jnp.int32, (rows, LANES), 1)
    block_end = jnp.where(col < SEL_BLOCK, col * SEL_BLOCK + (CMP_BLOCK - 1),
                          (col - SEL_BLOCK) * SEL_BLOCK + (SEL_BLOCK - 1))
    scol = lax.broadcasted_iota(jnp.int32, (NSA_KV_HEADS * t, page), 1)
    rcol = lax.broadcasted_iota(jnp.int32, (NSA_KV_HEADS * t, wlen), 1)
    dist = wlen + trow2 - rcol
    bias_w = _per_group_rows(jnp.where((dist >= 0) & (dist < WINDOW), 0.0, NEG_INF), t)
    dist_n = trow2 - scol
    bias_wn = _per_group_rows(jnp.where((dist_n >= 0) & (dist_n < WINDOW) & (scol < t), 0.0, NEG_INF), t)
    low = lax.broadcasted_iota(jnp.int32, (t, LANES), 1) < HEAD_DIM

    def body(k):
        rs = slice(k * t, (k + 1) * t)
        pages = all_pages[k * n_pages:(k + 1) * n_pages]
        qc = _stack_nsa_heads(nqc_ref, rs)
        qr = _stack_nsa_heads(nqr_ref, rs)
        for p in range(n_pages):
            ckall_ref[k, :, p * page:(p + 1) * page] = (pages[p][0, 0, 0:128, :] * phi_ref[0:1, :]).astype(BF16)
            cvall_ref[k, :, p * page:(p + 1) * page] = (pages[p][0, 0, 128:256, :] * phi_ref[1:2, :]).astype(BF16)
        newpad_ref[k] = jnp.zeros(newpad_ref.shape[1:], F32)
        newpad_ref[k, 0:t, :] = new_ref[rs, :]
        winpad_ref[k] = jnp.zeros(winpad_ref.shape[1:], F32)
        winpad_ref[k, 0:t, :] = winn_ref[rs, :]
        yield
        kcmp_t = _dot(ckall_ref[k], seg_ref[...]).astype(BF16)
        vcmp_t = _dot(cvall_ref[k], seg_ref[...]).astype(BF16)
        yield
        s = jnp.where(block_end <= qpos, _dot(qc, kcmp_t), NEG_INF)
        m = jnp.max(s, axis=1, keepdims=True)
        m = jnp.where(m == NEG_INF, 0.0, m)
        p_c = jnp.exp(s - m)
        p_c = p_c / jnp.maximum(jnp.sum(p_c, axis=1, keepdims=True), 1e-30)
        yield
        o_c = _dot_nt(p_c.astype(BF16), vcmp_t)
        imp = []
        for g in range(NSA_KV_HEADS):
            acc = p_c[(g * NSA_GROUP) * t:(g * NSA_GROUP + 1) * t]
            for c in range(1, NSA_GROUP):
                acc = acc + p_c[(g * NSA_GROUP + c) * t:(g * NSA_GROUP + c + 1) * t]
            imp.append(acc)
        imp = jnp.concatenate(imp, axis=0)
        imp = imp + pltpu.roll(imp, SEL_BLOCK, 1)
        sel = _select_blocks(imp, qpos2, n_sel)
        yield
        member = _dot(sel, _expand_matrix(LANES, n_pages * page, 0))
        member_new = _dot(sel, _expand_matrix(LANES, page, n_pages * page))
        raw = [_dot(qr, pages[p][0, 0, 256:384, :].astype(BF16)) for p in range(n_pages)]
        raw_new = _dot_nt(qr, newpad_ref[k, :, 256:384].astype(BF16))
        raw_w = _dot(qr, swin_ref[0, k, 0:128, :].astype(BF16))
        raw_wn = _dot_nt(qr, winpad_ref[k, :, 0:128].astype(BF16))
        yield
        bias_past = _per_group_rows(jnp.where(member > 0.5, 0.0, NEG_INF), t)
        ok_new = (member_new > 0.5) & (scol <= trow2) & (scol < t)
        bias_new = _per_group_rows(jnp.where(ok_new, 0.0, NEG_INF), t)
        parts = [(raw[p] + bias_past[:, p * page:(p + 1) * page], pages[p][0, 0, 384:512, :].astype(BF16), True)
                 for p in range(n_pages)]
        parts.append((raw_new + bias_new, newpad_ref[k, :, 384:512].astype(BF16), False))
        o_s = _softmax_parts(parts)
        yield
        o_w = _softmax_parts([(raw_w + bias_w, swin_ref[0, k, 128:256, :].astype(BF16), True),
                              (raw_wn + bias_wn, winpad_ref[k, :, 128:256].astype(BF16), False)])
        yield
        gate = 1.0 / (1.0 + jnp.exp(-bg_ref[rs, :]))
        for c in range(NSA_GROUP):
            per_group = []
            for g in range(NSA_KV_HEADS):
                r0 = (g * NSA_GROUP + c) * t
                cg = (g * NSA_GROUP + c) * 3
                per_group.append(gate[:, cg:cg + 1] * o_c[r0:r0 + t] + gate[:, cg + 1:cg + 2] * o_s[r0:r0 + t]
                                 + gate[:, cg + 2:cg + 3] * o_w[r0:r0 + t])
            mixed = jnp.where(low, per_group[0], per_group[1])
            sl = slice(c * LANES, (c + 1) * LANES)
            o_ref[rs, sl] = mixed * _silu(ng_ref[rs, sl])

    _side_by_side(body, group)


def _nsa_sample(page_table, nqc, nqr, nsa_new, win_new, bgate, ngate, state_win_t, phi_rows, seg, cache_t,
                layer, row0, dec_batch, t, pos0):
    n_pages = page_table.shape[1]
    page = cache_t.shape[3]
    group = SAMPLE_GROUP
    gt = group * t
    blk0 = row0 // gt
    own = lambda w: pl.BlockSpec((gt, w), lambda b, pt: (b, 0))
    joint = lambda w: pl.BlockSpec((gt, w), lambda b, pt: (blk0 + b, 0))
    wlen = state_win_t.shape[3]
    n_keys = n_pages * page
    grid_spec = pltpu.PrefetchScalarGridSpec(
        num_scalar_prefetch=1,
        grid=(dec_batch // group,),
        in_specs=[own(512), own(512), joint(512), joint(256), own(128), joint(512),
                  pl.BlockSpec((1, group, 256, wlen), lambda b, pt: (layer, b, 0, 0)),
                  pl.BlockSpec((2, page), lambda b, pt: (0, 0)),
                  pl.BlockSpec((n_keys, LANES), lambda b, pt: (0, 0))]
        + _page_specs(layer, n_pages, page, group),
        out_specs=pl.BlockSpec((gt, 512), lambda b, pt: (b, 0)),
        scratch_shapes=[pltpu.VMEM((group, LANES, n_keys), BF16), pltpu.VMEM((group, LANES, n_keys), BF16),
                        pltpu.VMEM((group, page, 512), F32), pltpu.VMEM((group, page, 256), F32)],
    )
    return pl.pallas_call(
        functools.partial(_nsa_sample_kernel, n_pages=n_pages, page=page, pos0=pos0, group=group),
        grid_spec=grid_spec,
        out_shape=jax.ShapeDtypeStruct((dec_batch * t, 512), F32),
        compiler_params=pltpu.CompilerParams(dimension_semantics=("parallel",), vmem_limit_bytes=VMEM_LIMIT),
        name="nsa_sample",
    )(page_table, nqc, nqr, nsa_new, win_new, bgate, ngate, state_win_t, phi_rows, seg,
      *([cache_t] * (n_pages * group)))


def _out_proj_kernel(oap_ref, obp_ref, ocp_ref, oas_ref, obs_ref, ocs_ref, x_ref, w_ref, g_ref, y_ref, *,
                     n_prompt_tiles):
    prompt = pl.program_id(0) < n_prompt_tiles
    pick = lambda p_ref, s_ref: jnp.where(prompt, p_ref[...], s_ref[...]).astype(BF16)
    out = (_dot(pick(oap_ref, oas_ref), w_ref[0:256, :])
           + _dot(pick(obp_ref, obs_ref), w_ref[256:512, :])
           + _dot(pick(ocp_ref, ocs_ref), w_ref[512:1024, :]))
    ms = jnp.mean(out * out, axis=-1, keepdims=True)
    y_ref[...] = x_ref[...] + out * lax.rsqrt(ms + NORM_EPS) * g_ref[...]


def _out_proj(prompt_parts, sample_parts, x, w_out_p, g_post, n_p):
    n = x.shape[0]
    tm = ROW_TILE
    npt = n_p // tm
    row = lambda w: pl.BlockSpec((tm, w), lambda i: (i, 0))
    prompt_row = lambda w: pl.BlockSpec((tm, w), lambda i: (jnp.minimum(i, npt - 1), 0))
    sample_row = lambda w: pl.BlockSpec((tm, w), lambda i: (jnp.maximum(i - npt, 0), 0))
    fixed = lambda s: pl.BlockSpec(s, lambda i: (0, 0), pipeline_mode=pl.Buffered(1))
    widths = (256, 256, 512)
    return pl.pallas_call(
        functools.partial(_out_proj_kernel, n_prompt_tiles=npt),
        grid=(n // tm,),
        in_specs=[prompt_row(w) for w in widths] + [sample_row(w) for w in widths]
        + [row(D_MODEL), fixed((D_MODEL, D_MODEL)), fixed((1, D_MODEL))],
        out_specs=row(D_MODEL),
        out_shape=jax.ShapeDtypeStruct((n, D_MODEL), F32),
        compiler_params=pltpu.CompilerParams(dimension_semantics=("arbitrary",), vmem_limit_bytes=VMEM_LIMIT),
        name="out_proj",
    )(*prompt_parts, *sample_parts, x, w_out_p, g_post)


def _win_update_kernel(state_ref, new_ref, o_ref, *, nb, t):
    wlen = state_ref.shape[3]
    for bi in range(nb):
        o_ref[0, bi] = pltpu.roll(state_ref[0, bi], wlen - t, 1)
        o_ref[0, bi, :, wlen - t:wlen] = new_ref[0, bi * t:(bi + 1) * t, :].T


def _win_update(state_win_t, win_rows, t):
    depth, bs, width, wlen = state_win_t.shape
    nb = 8
    return pl.pallas_call(
        functools.partial(_win_update_kernel, nb=nb, t=t),
        grid=(depth, bs // nb),
        in_specs=[pl.BlockSpec((1, nb, width, wlen), lambda l, b: (l, b, 0, 0)),
                  pl.BlockSpec((1, nb * t, width), lambda l, b: (l, b, 0))],
        out_specs=pl.BlockSpec((1, nb, width, wlen), lambda l, b: (l, b, 0, 0)),
        out_shape=jax.ShapeDtypeStruct(state_win_t.shape, F32),
        compiler_params=pltpu.CompilerParams(dimension_semantics=("parallel", "parallel"),
                                             vmem_limit_bytes=VMEM_LIMIT),
        name="win_update",
    )(state_win_t, win_rows)


def _to_chunk_layout(w, axis):
    shape = w.shape
    split = shape[:axis] + (NSA_KV_HEADS, NSA_GROUP, HEAD_DIM) + shape[axis + 1:]
    return jnp.swapaxes(w.reshape(split), axis, axis + 1).reshape(shape)


def _rope_tables(pos):
    inv = ROPE_THETA ** (-jnp.arange(ROT_HALF, dtype=F32) * (2.0 / ROT_DIM))
    ang = pos.astype(F32)[:, None] * inv[None, :]
    cos, sin = jnp.cos(ang), jnp.sin(ang)
    n = pos.shape[0]
    cos_h = jnp.concatenate([cos, cos, jnp.ones((n, HEAD_DIM - ROT_DIM), F32)], axis=1)
    sin_h = jnp.concatenate([-sin, sin, jnp.zeros((n, HEAD_DIM - ROT_DIM), F32)], axis=1)
    reps = (1, LANES // HEAD_DIM)
    return jnp.tile(cos_h, reps), jnp.tile(sin_h, reps), cos.T, sin.T


def _segment_matrix(n_keys):
    n = np.arange(n_keys) // CMP_BLOCK
    col = (n % 2) * SEL_BLOCK + n // 2
    return jnp.asarray(col[:, None] == np.arange(LANES)[None, :], BF16)


def kernel(x_prompt, x_sample, cache_sb_kv, cache_nsa_kv, state_win_kv, state_pool, page_table, norm_pre, w_in,
           pool_w, pool_scale, phi_k, phi_v, w_out, norm_post):
    bp, tp, d = x_prompt.shape
    bs, ts, _ = x_sample.shape
    depth = w_in.shape[0]
    n_pool, page = cache_sb_kv.shape[1], cache_sb_kv.shape[2]
    past_len = page_table.shape[1] * page
    wlen = state_win_kv.shape[2]
    n_p, n_s = bp * tp, bs * ts
    assert d == D_MODEL and n_p % ROW_TILE == 0 and n_s % ROW_TILE == 0
    assert tp % K_TILE == 0 and K_TILE % Q_TILE == 0 and tp % SEL_BLOCK == 0 and n_p % n_s == 0
    assert past_len % SEL_BLOCK == 0 and ts < CMP_BLOCK and ts % 8 == 0 and page == LANES
    assert past_len // CMP_BLOCK <= SEL_BLOCK and (wlen >= WINDOW or wlen == past_len)

    o = np.cumsum((0,) + (256,) * 6 + (512,) + (128,) * 6 + (24, 512)).tolist()
    w_rows = jnp.swapaxes(w_in, 1, 2)
    w_t = jnp.concatenate([
        w_rows[:, o[0]:o[1]], _to_chunk_layout(w_rows[:, o[6]:o[7]], 1), w_rows[:, o[13]:o[14]],
        jnp.zeros((depth, IN_WIDTH_P - o[15], D_MODEL), w_in.dtype),
        w_rows[:, o[1]:o[6]], w_rows[:, o[7]:o[13]], _to_chunk_layout(w_rows[:, o[14]:o[15]], 1)],
        axis=1).astype(BF16)
    w_out_p = jnp.concatenate([w_out[:, :512], _to_chunk_layout(w_out[:, 512:], 1)], axis=1).astype(BF16)
    w_pool = jnp.zeros((depth, POOL_WIDTH, POOL_WIDTH), F32)
    for g in range(len(POOL_WINDOWS)):
        sl = slice(g * POOL_GDIM, (g + 1) * POOL_GDIM)
        w_pool = w_pool.at[:, sl, sl].set(pool_w[:, g])
    w_pool = w_pool.astype(BF16)
    reps = page // CMP_BLOCK
    phi_rows = jnp.stack([jnp.tile(phi_k, (1, reps)), jnp.tile(phi_v, (1, reps))], axis=1)
    seg = _segment_matrix(past_len)

    pos = jnp.concatenate([jnp.tile(jnp.arange(tp, dtype=jnp.int32), bp),
                           jnp.tile(past_len + jnp.arange(ts, dtype=jnp.int32), bs)])
    tables = _rope_tables(pos)

    cache_sb_t = jnp.transpose(cache_sb_kv, (0, 1, 3, 4, 5, 2)).reshape(depth, n_pool, 512, page)
    cache_nsa_t = jnp.transpose(cache_nsa_kv, (0, 1, 3, 4, 5, 2)).reshape(depth, n_pool, 512, page)
    state_win_t = jnp.transpose(state_win_kv, (0, 1, 3, 4, 5, 2)).reshape(depth, bs, 256, wlen)

    x = jnp.concatenate([x_prompt.reshape(n_p, d), x_sample.reshape(n_s, d)], axis=0)
    sbkv_all = jnp.zeros((depth, bp, 512, tp), F32)
    nsa_all = jnp.zeros((depth, bp, 512, tp), F32)
    keep_p = min(WINDOW, tp)
    win_p, pool_p, sb_s, nsa_s, win_rows, pool_s = [], [], [], [], [], []
    for layer in range(depth):
        (sbkv, sbg, pu, pg, nsa_new, win_new, ngate, cmp, sbq, nqc, nqr, bgate,
         sbqt, nqct, nqrt, bgt, sbkv_all, nsa_all, win_t) = _in_proj(
            x, norm_pre[layer][None], w_t[layer], phi_k[layer], phi_v[layer], tables, sbkv_all, nsa_all,
            layer, n_p, bp, tp)
        pool_args = (pu, pg, w_pool[layer], pool_scale[layer][None])
        prompt_parts = (
            _sb_prompt(sbqt, sbkv, sbkv_all, sbg, layer, bp, tp),
            _pool(*pool_args, None, layer, 0, 1, tp, bp, 0),
            _nsa_prompt(nqct, nqrt, bgt, ngate, cmp, nsa_new, nsa_all, win_new, win_t, layer, bp, tp))
        sample_parts = (
            _sb_sample(page_table, sbq, sbkv, sbg, cache_sb_t, layer, n_p, bs, ts),
            _pool(*pool_args, state_pool, layer, n_p, bs, ts, bs, past_len),
            _nsa_sample(page_table, nqc, nqr, nsa_new, win_new, bgate, ngate, state_win_t, phi_rows[layer], seg,
                        cache_nsa_t, layer, n_p, bs, ts, past_len))
        x = _out_proj(prompt_parts, sample_parts, x, w_out_p[layer], norm_post[layer][None], n_p)

        win_p.append(win_t[:, :, tp - keep_p:])
        pool_p.append(pu[:n_p].reshape(bp, tp, POOL_WIDTH)[:, tp - POOL_BUF:])
        sb_s.append(sbkv[n_p:].reshape(bs, ts, 2, SB_HEADS, HEAD_DIM))
        nsa_s.append(nsa_new[n_p:].reshape(bs, ts, 4, NSA_KV_HEADS, HEAD_DIM))
        win_rows.append(win_new[n_p:])
        pool_s.append(jnp.concatenate([state_pool[layer], pu[n_p:].reshape(bs, ts, 256)], axis=1)[:, -POOL_BUF:])
    win_s_t = _win_update(state_win_t, jnp.stack(win_rows), ts)

    def token_major(a, n_kv, n_heads):
        lead = a.shape[:-2]
        a = a.reshape(lead + (n_kv, n_heads, HEAD_DIM, a.shape[-1]))
        return jnp.moveaxis(a, -1, len(lead))

    y_prompt = x[:n_p].reshape(bp, tp, d)
    y_sample = x[n_p:].reshape(bs, ts, d)
    return (y_prompt, y_sample,
            token_major(sbkv_all, 2, SB_HEADS), token_major(nsa_all, 4, NSA_KV_HEADS),
            token_major(jnp.stack(win_p), 2, NSA_KV_HEADS), jnp.stack(pool_p),
            jnp.stack(sb_s), jnp.stack(nsa_s), token_major(win_s_t, 2, NSA_KV_HEADS), jnp.stack(pool_s))
```

```python
import functools

import jax
import jax.numpy as jnp
import numpy as np
from jax import lax
from jax.experimental import pallas as pl
from jax.experimental.pallas import tpu as pltpu

F32 = jnp.float32
BF16 = jnp.bfloat16

D_MODEL = 1024
HEAD_DIM = 64
SB_WIDTH = 256
SB_HEADS = 4
POOL_WINDOWS = (2, 4, 8, 16)
POOL_WIDTH = 256
POOL_GDIM = 64
POOL_BUF = 15
NSA_WIDTH = 512
NSA_HEADS = 8
NSA_KV_HEADS = 2
NSA_GROUP = 4
NSA_KV_WIDTH = 128
CMP_BLOCK = 32
SEL_BLOCK = 64
SEL_TOPN = 8
WINDOW = 512
ROPE_THETA = 500000.0
ROT_DIM = 16
ROT_HALF = ROT_DIM // 2
NORM_EPS = 1e-6
FORCED_SCORE = 1e4
SCALE = HEAD_DIM ** -0.5
LOG2E = 1.4426950408889634
NEG_INF = float("-inf")
MASKED = -(2.0 ** 100)

C_SBQ, C_NQ, C_BG, C_QEND = 0, 256, 768, 896
C_SBK, C_SBG, C_PU, C_PG = 896, 1408, 1664, 1920
C_NSA = 2176
C_WIN = 2688
C_NG = 2944
IN_WIDTH_P = 3456
LANES = 128
ROW_TILE = 512
Q_TILE = 128
K_TILE = 256
SAMPLE_GROUP = 4
VMEM_LIMIT = 56 * 1024 * 1024


def _dot(a, b):
    return jnp.dot(a, b, preferred_element_type=F32)


def _dot_nt(a, b):
    return lax.dot_general(a, b, (((1,), (1,)), ((), ())), preferred_element_type=F32)


def _dot_tn(a, b):
    return lax.dot_general(a, b, (((0,), (0,)), ((), ())), preferred_element_type=F32)


def _silu(g):
    return g / (1.0 + jnp.exp(-g))


def _softplus(z):
    return jnp.maximum(z, 0.0) + jnp.log(1.0 + jnp.exp(-jnp.abs(z)))


def _split_bf16(x):
    hi = x.astype(BF16)
    return hi, (x - hi.astype(F32)).astype(BF16)


def _suffix_sum_exclusive(x, u_bf16):
    hi, lo = _split_bf16(x)
    n = x.shape[0]
    both = _dot(jnp.concatenate([hi, lo], axis=0), u_bf16)
    return both[:n] + both[n:]


def _later_matrix(n, transposed=False):
    r = lax.broadcasted_iota(jnp.int32, (n, n), 0)
    c = lax.broadcasted_iota(jnp.int32, (n, n), 1)
    return ((c > r) if transposed else (r > c)).astype(BF16)


def _in_proj_kernel(phik_ref, phiv_ref, x_ref, g_ref, wt_ref, cos_ref, sin_ref, cost_ref, sint_ref,
                    sbkv_prev_ref, nsa_prev_ref,
                    sbkv_ref, sbg_ref, pu_ref, pg_ref, nsa_ref, win_ref, ng_ref, cmp_ref,
                    sbq_ref, nqc_ref, nqr_ref, bg_ref,
                    sbqt_ref, nqct_ref, nqrt_ref, bgt_ref, sbkvt_ref, nsat_ref, wint_ref,
                    ck_ref, cv_ref, *, n_prompt_tiles):
    del sbkv_prev_ref, nsa_prev_ref
    i = pl.program_id(0)
    x = x_ref[...]
    ms = jnp.mean(x * x, axis=-1, keepdims=True)
    h = (x * lax.rsqrt(ms + NORM_EPS) * g_ref[...]).astype(BF16)

    def proj(a, b):
        return _dot_nt(h, wt_ref[a:b, :])

    cos = cos_ref[...]
    sin = sin_ref[...]
    first = (lax.broadcasted_iota(jnp.int32, cos.shape, 1) % HEAD_DIM) < ROT_HALF

    def rope(v):
        swapped = jnp.where(first, pltpu.roll(v, LANES - ROT_HALF, 1), pltpu.roll(v, ROT_HALF, 1))
        return v * cos + swapped * sin

    sbkv_ref[...] = proj(C_SBK, C_SBK + 512)
    gates = proj(C_SBG, C_SBG + 768)
    sbg_ref[...] = gates[:, 0:256]
    pu_ref[...] = gates[:, 256:512]
    pg_ref[...] = gates[:, 512:768]
    ng_ref[...] = proj(C_NG, C_NG + 512)
    nsa = proj(C_NSA, C_NSA + 512)
    ck_ref[...] = nsa[:, 0:128]
    cv_ref[...] = nsa[:, 128:256]
    nsa_ref[:, 0:256] = nsa[:, 0:256]
    nsa_ref[:, 256:384] = rope(nsa[:, 256:384])
    nsa_ref[:, 384:512] = nsa[:, 384:512]
    win = proj(C_WIN, C_WIN + 256)
    win_ref[:, 0:128] = rope(win[:, 0:128])
    win_ref[:, 128:256] = win[:, 128:256]

    nb = cmp_ref.shape[0]
    for part, (src_ref, phi_ref) in enumerate(((ck_ref, phik_ref), (cv_ref, phiv_ref))):
        acc_e = jnp.zeros((nb, LANES), F32)
        acc_o = jnp.zeros((nb, LANES), F32)
        for l in range(CMP_BLOCK):
            acc_e = acc_e + src_ref[pl.ds(l, nb, stride=SEL_BLOCK), :] * phi_ref[l]
            acc_o = acc_o + src_ref[pl.ds(CMP_BLOCK + l, nb, stride=SEL_BLOCK), :] * phi_ref[l]
        cmp_ref[:, part * 256:part * 256 + 128] = acc_e
        cmp_ref[:, part * 256 + 128:part * 256 + 256] = acc_o

    @pl.when(i >= n_prompt_tiles)
    def _():
        q_all = proj(C_SBQ, C_QEND)
        sbq_ref[...] = q_all[:, C_SBQ:C_SBQ + 256]
        bg_ref[...] = q_all[:, C_BG:C_BG + 128]
        nqc_ref[...] = q_all[:, C_NQ:C_NQ + 512]
        for c in range(NSA_GROUP):
            nqr_ref[:, c * LANES:(c + 1) * LANES] = rope(q_all[:, C_NQ + c * LANES:C_NQ + (c + 1) * LANES])

    @pl.when(i < n_prompt_tiles)
    def _():
        cos_t = cost_ref[...]
        sin_t = sint_ref[...]

        def rope_t(v):
            pieces = []
            for base in range(0, LANES, HEAD_DIM):
                x1 = v[base:base + ROT_HALF]
                x2 = v[base + ROT_HALF:base + ROT_DIM]
                pieces += [x1 * cos_t - x2 * sin_t, x2 * cos_t + x1 * sin_t, v[base + ROT_DIM:base + HEAD_DIM]]
            return jnp.concatenate(pieces, axis=0)

        qt_all = _dot_nt(wt_ref[C_SBQ:C_QEND, :], h)
        sbqt_ref[...] = qt_all[C_SBQ:C_SBQ + 256]
        bgt_ref[...] = qt_all[C_BG:C_BG + 128]
        nqct_ref[...] = qt_all[C_NQ:C_NQ + 512]
        for c in range(NSA_GROUP):
            nqrt_ref[c * LANES:(c + 1) * LANES, :] = rope_t(qt_all[C_NQ + c * LANES:C_NQ + (c + 1) * LANES])
        sbkvt_ref[0, 0] = sbkv_ref[...].T
        nsat_ref[0, 0] = nsa_ref[...].T
        wint_ref[0] = win_ref[...].T


def _in_proj(x, g_pre, w_t, phi_k, phi_v, tables, sbkv_all, nsa_all, layer, n_p, batch, seq):
    n = x.shape[0]
    n_s = n - n_p
    tm = ROW_TILE
    npt = n_p // tm
    per_seq = seq // tm
    cos_r, sin_r, cos_t, sin_t = tables
    row = lambda w: pl.BlockSpec((tm, w), lambda i: (i, 0))
    fixed = lambda s: pl.BlockSpec(s, lambda i: (0, 0), pipeline_mode=pl.Buffered(1))
    smem = pl.BlockSpec(memory_space=pltpu.SMEM)
    anywhere = pl.BlockSpec(memory_space=pl.ANY)
    prompt_tile = lambda i: jnp.minimum(i, npt - 1)
    sample_row = lambda w: pl.BlockSpec((tm, w), lambda i: (jnp.maximum(i - npt, 0), 0))
    prompt_col = lambda f: pl.BlockSpec((f, tm), lambda i: (0, prompt_tile(i)))
    stacked = pl.BlockSpec((1, 1, 512, tm), lambda i: (layer, prompt_tile(i) // per_seq, 0, prompt_tile(i) % per_seq))
    per_batch = pl.BlockSpec((1, 256, tm), lambda i: (prompt_tile(i) // per_seq, 0, prompt_tile(i) % per_seq))
    all_widths = (512, 256, 256, 256, 512, 256, 512)
    sample_widths = (256, 512, 512, 128)
    prompt_feats = (256, 512, 512, 128)
    out_shape = ([jax.ShapeDtypeStruct((n, w), F32) for w in all_widths]
                 + [jax.ShapeDtypeStruct((n // SEL_BLOCK, 512), F32)]
                 + [jax.ShapeDtypeStruct((n_s, w), F32) for w in sample_widths]
                 + [jax.ShapeDtypeStruct((f, n_p), F32) for f in prompt_feats]
                 + [jax.ShapeDtypeStruct(sbkv_all.shape, F32), jax.ShapeDtypeStruct(nsa_all.shape, F32),
                    jax.ShapeDtypeStruct((batch, 256, seq), F32)])
    out_specs = ([row(w) for w in all_widths] + [pl.BlockSpec((tm // SEL_BLOCK, 512), lambda i: (i, 0))]
                 + [sample_row(w) for w in sample_widths] + [prompt_col(f) for f in prompt_feats]
                 + [stacked, stacked, per_batch])
    n_fixed_outs = len(all_widths) + 1 + len(sample_widths) + len(prompt_feats)
    return pl.pallas_call(
        functools.partial(_in_proj_kernel, n_prompt_tiles=npt),
        grid=(n // tm,),
        in_specs=[smem, smem, row(D_MODEL), fixed((1, D_MODEL)), fixed((IN_WIDTH_P, D_MODEL)), row(LANES), row(LANES),
                  pl.BlockSpec((ROT_HALF, tm), lambda i: (0, i)), pl.BlockSpec((ROT_HALF, tm), lambda i: (0, i)),
                  anywhere, anywhere],
        out_specs=out_specs,
        out_shape=out_shape,
        input_output_aliases={9: n_fixed_outs, 10: n_fixed_outs + 1},
        scratch_shapes=[pltpu.VMEM((tm, LANES), F32), pltpu.VMEM((tm, LANES), F32)],
        compiler_params=pltpu.CompilerParams(dimension_semantics=("arbitrary",), vmem_limit_bytes=VMEM_LIMIT),
        name="in_proj",
    )(phi_k, phi_v, x, g_pre, w_t, cos_r, sin_r, cos_t, sin_t, sbkv_all, nsa_all)


def _sb_prompt_kernel(qt_ref, k_ref, vt_ref, g_ref, o_ref, acc_ref, run_ref):
    tq = qt_ref.shape[1]
    tk = tq
    i = pl.program_id(1)
    n_chains = acc_ref.shape[0]
    tc = tq // n_chains
    n_rows = SB_HEADS * tc
    qt = qt_ref[...] * SCALE
    head = lax.broadcasted_iota(jnp.int32, (SB_WIDTH, tc), 0) // HEAD_DIM
    q4t = [jnp.concatenate([jnp.where(head == h, qt[:, c * tc:(c + 1) * tc], 0.0) for h in range(SB_HEADS)],
                           axis=1).astype(BF16) for c in range(n_chains)]
    u_t = _later_matrix(tk, transposed=True)
    qrel = lax.broadcasted_iota(jnp.int32, (tk, n_rows), 1) % tc
    krow = lax.broadcasted_iota(jnp.int32, (tk, n_rows), 0)
    acc_ref[...] = jnp.zeros_like(acc_ref)
    run_ref[...] = jnp.zeros_like(run_ref)

    def step(j, diagonal):
        start = pl.multiple_of(j * tk, tk)
        k = k_ref[pl.ds(start, tk), :].astype(BF16)
        vt = vt_ref[0, 0, :, pl.ds(start, tk)].astype(BF16)
        chains = range(n_chains)
        zs = [_dot(k, q4t[c]) for c in chains]
        log_keeps = [-_softplus(z) for z in zs]
        if diagonal:
            masks = [krow < qrel + c * tc for c in chains]
            log_keeps = [jnp.where(m, lk, 0.0) for m, lk in zip(masks, log_keeps)]
        boths = [_dot(u_t, jnp.concatenate(_split_bf16(lk), axis=1)) for lk in log_keeps]
        runs = [run_ref[c] for c in chains]
        a_s = [jnp.exp(z + lk + (both[:, :n_rows] + both[:, n_rows:] + run))
               for z, lk, both, run in zip(zs, log_keeps, boths, runs)]
        if diagonal:
            a_s = [jnp.where(m, a, 0.0) for m, a in zip(masks, a_s)]
        pvs = [_dot(vt, a.astype(BF16)) for a in a_s]
        for c in chains:
            acc_ref[c] += pvs[c]
            run_ref[c] = runs[c] + jnp.sum(log_keeps[c], axis=0, keepdims=True)

    step(i, True)

    def body(jj, carry):
        step(i - 1 - jj, False)
        return carry

    lax.fori_loop(0, i, body, 0)
    for c in range(n_chains):
        acc = acc_ref[c]
        out_t = jnp.concatenate([acc[h * HEAD_DIM:(h + 1) * HEAD_DIM, h * tc:(h + 1) * tc] for h in range(SB_HEADS)],
                                axis=0)
        o_ref[c * tc:(c + 1) * tc, :] = out_t.T * _silu(g_ref[c * tc:(c + 1) * tc, :])


def _sb_prompt(sbqt, sbkv, sbkv_all, sbg, layer, batch, seq):
    tq = K_TILE
    n_chains = K_TILE // Q_TILE
    nq = seq // tq
    return pl.pallas_call(
        _sb_prompt_kernel,
        grid=(batch, nq),
        in_specs=[pl.BlockSpec((256, tq), lambda b, i: (0, b * nq + i)),
                  pl.BlockSpec((seq, 256), lambda b, i: (b, 0)),
                  pl.BlockSpec((1, 1, 256, seq), lambda b, i: (layer, b, 1, 0)),
                  pl.BlockSpec((tq, 256), lambda b, i: (b * nq + i, 0))],
        out_specs=pl.BlockSpec((tq, 256), lambda b, i: (b * nq + i, 0)),
        out_shape=jax.ShapeDtypeStruct((batch * seq, 256), F32),
        scratch_shapes=[pltpu.VMEM((n_chains, SB_WIDTH, SB_HEADS * Q_TILE), F32),
                        pltpu.VMEM((n_chains, 1, SB_HEADS * Q_TILE), F32)],
        compiler_params=pltpu.CompilerParams(dimension_semantics=("parallel", "parallel"),
                                             vmem_limit_bytes=VMEM_LIMIT),
        name="sb_prompt",
    )(sbqt, sbkv, sbkv_all, sbg)


def _stack_sb_heads(q):
    head = lax.broadcasted_iota(jnp.int32, q.shape, 1) // HEAD_DIM
    return jnp.concatenate([jnp.where(head == h, q, 0.0) for h in range(SB_HEADS)], axis=0).astype(BF16)


def _unstack_sb_heads(acc, t):
    head = lax.broadcasted_iota(jnp.int32, (t, SB_WIDTH), 1) // HEAD_DIM
    out = jnp.zeros((t, SB_WIDTH), F32)
    for h in range(SB_HEADS):
        out = jnp.where(head == h, acc[h * t:(h + 1) * t], out)
    return out


def _side_by_side(make_body, n):
    live = [make_body(k) for k in range(n)]
    while live:
        still = []
        for gen in live:
            try:
                next(gen)
                still.append(gen)
            except StopIteration:
                pass
        live = still


def _sb_sample_kernel(pt_ref, q_ref, kvn_ref, g_ref, *rest, n_pages, page, group):
    all_pages = rest[:n_pages * group]
    o_ref = rest[n_pages * group]
    new_ref = rest[n_pages * group + 1]
    t = q_ref.shape[0] // group
    rows = SB_HEADS * t
    u = _later_matrix(page)
    trow = lax.broadcasted_iota(jnp.int32, (rows, page), 0) % t
    scol = lax.broadcasted_iota(jnp.int32, (rows, page), 1)
    mask = scol < trow
    order = list(reversed(range(n_pages)))

    def body(k):
        rs = slice(k * t, (k + 1) * t)
        pages = all_pages[k * n_pages:(k + 1) * n_pages]
        q4 = _stack_sb_heads(q_ref[rs, :] * SCALE)
        new_ref[k] = jnp.zeros(new_ref.shape[1:], F32)
        new_ref[k, 0:t, :] = kvn_ref[rs, :]
        z_new = _dot_nt(q4, new_ref[k, :, 0:256].astype(BF16))
        yield
        zs = [z_new] + [_dot(q4, pages[p][0, 0, 0:256, :].astype(BF16)) for p in order]
        yield
        log_keeps = [jnp.where(mask, -_softplus(z_new), 0.0)] + [-_softplus(z) for z in zs[1:]]
        halves = []
        for lk in log_keeps:
            halves += list(_split_bf16(lk))
        yield
        both = _dot(jnp.concatenate(halves, axis=0), u)
        yield
        run = jnp.zeros((rows, 1), F32)
        acc = jnp.zeros((rows, SB_WIDTH), F32)
        for idx, (z, lk) in enumerate(zip(zs, log_keeps)):
            later = both[2 * idx * rows:(2 * idx + 1) * rows] + both[(2 * idx + 1) * rows:(2 * idx + 2) * rows] + run
            a = jnp.exp(z + lk + later)
            if idx == 0:
                acc = acc + _dot(jnp.where(mask, a, 0.0).astype(BF16), new_ref[k, :, 256:512].astype(BF16))
            else:
                acc = acc + _dot_nt(a.astype(BF16), pages[order[idx - 1]][0, 0, 256:512, :].astype(BF16))
            run = run + jnp.sum(lk, axis=1, keepdims=True)
        yield
        o_ref[rs, :] = _unstack_sb_heads(acc, t) * _silu(g_ref[rs, :])

    _side_by_side(body, group)


def _page_specs(layer, n_pages, page, group):
    def spec(k, p):
        return pl.BlockSpec((1, 1, 512, page), lambda b, pt: (layer, pt[b * group + k, p], 0, 0))
    return [spec(k, p) for k in range(group) for p in range(n_pages)]


def _sb_sample(page_table, sbq, sbkv, sbg, cache_t, layer, row0, dec_batch, t):
    n_pages = page_table.shape[1]
    page = cache_t.shape[3]
    group = SAMPLE_GROUP
    gt = group * t
    blk0 = row0 // gt
    own = lambda w: pl.BlockSpec((gt, w), lambda b, pt: (b, 0))
    joint = lambda w: pl.BlockSpec((gt, w), lambda b, pt: (blk0 + b, 0))
    grid_spec = pltpu.PrefetchScalarGridSpec(
        num_scalar_prefetch=1,
        grid=(dec_batch // group,),
        in_specs=[own(256), joint(512), joint(256)] + _page_specs(layer, n_pages, page, group),
        out_specs=pl.BlockSpec((gt, 256), lambda b, pt: (b, 0)),
        scratch_shapes=[pltpu.VMEM((group, page, 512), F32)],
    )
    return pl.pallas_call(
        functools.partial(_sb_sample_kernel, n_pages=n_pages, page=page, group=group),
        grid_spec=grid_spec,
        out_shape=jax.ShapeDtypeStruct((dec_batch * t, 256), F32),
        compiler_params=pltpu.CompilerParams(dimension_semantics=("parallel",), vmem_limit_bytes=VMEM_LIMIT),
        name="sb_sample",
    )(page_table, sbq, sbkv, sbg, *([cache_t] * (n_pages * group)))


def _pool_kernel(u_ref, g_ref, w_ref, sc_ref, *rest, nb, t, pos0, has_prefix):
    if has_prefix:
        pre_ref, o_ref, rows_ref = rest
    else:
        o_ref, rows_ref = rest
    pad = POOL_BUF + 1
    u = u_ref[...].reshape(nb, t, POOL_WIDTH)
    rows_ref[:, 0:pad, :] = jnp.zeros((nb, pad, POOL_WIDTH), F32)
    if has_prefix:
        rows_ref[:, 1:pad, :] = pre_ref[0]
    rows_ref[:, pad:pad + t, :] = u
    back = lambda s: rows_ref[:, pad - s:pad - s + t, :]
    sums = {}
    total = u
    done = 1
    for w in POOL_WINDOWS:
        for s in range(done, w):
            total = total + back(s)
        done = w
        sums[w] = total
    lane = lax.broadcasted_iota(jnp.int32, (nb, t, POOL_WIDTH), 2) // POOL_GDIM
    pos = pos0 + lax.broadcasted_iota(jnp.int32, (nb, t, POOL_WIDTH), 1)
    window = jnp.zeros((nb, t, POOL_WIDTH), jnp.int32)
    picked = jnp.zeros((nb, t, POOL_WIDTH), F32)
    for gi, w in enumerate(POOL_WINDOWS):
        window = jnp.where(lane == gi, w, window)
        picked = jnp.where(lane == gi, sums[w], picked)
    count = jnp.minimum(pos + 1, window).astype(F32)
    d = picked / count - u
    y = _dot(d.reshape(nb * t, POOL_WIDTH).astype(BF16), w_ref[...]) * sc_ref[...]
    o_ref[...] = y * _silu(g_ref[...])


def _pool(pu, pg, w_bd, scale, prefix, layer, row0, nb, t, n_seq, pos0):
    rows = nb * t
    blk0 = row0 // rows
    has_prefix = prefix is not None
    in_specs = [pl.BlockSpec((rows, 256), lambda i: (blk0 + i, 0)),
                pl.BlockSpec((rows, 256), lambda i: (blk0 + i, 0)),
                pl.BlockSpec((256, 256), lambda i: (0, 0)),
                pl.BlockSpec((1, 256), lambda i: (0, 0))]
    args = [pu, pg, w_bd, scale]
    if has_prefix:
        in_specs.append(pl.BlockSpec((1, nb, POOL_BUF, 256), lambda i: (layer, i, 0, 0)))
        args.append(prefix)
    return pl.pallas_call(
        functools.partial(_pool_kernel, nb=nb, t=t, pos0=pos0, has_prefix=has_prefix),
        grid=(n_seq // nb,),
        in_specs=in_specs,
        out_specs=pl.BlockSpec((rows, 256), lambda i: (i, 0)),
        out_shape=jax.ShapeDtypeStruct((n_seq * t, 256), F32),
        scratch_shapes=[pltpu.VMEM((nb, POOL_BUF + 1 + t, 256), F32)],
        compiler_params=pltpu.CompilerParams(dimension_semantics=("parallel",), vmem_limit_bytes=VMEM_LIMIT),
        name="pool_sample" if has_prefix else "pool_prompt",
    )(*args)


def _stack_nsa_heads_t(ref, cols, scale):
    t = cols.stop - cols.start
    low = lax.broadcasted_iota(jnp.int32, (LANES, t), 0) < HEAD_DIM
    g0, g1 = [], []
    for c in range(NSA_GROUP):
        ch = ref[c * LANES:(c + 1) * LANES, cols] * scale
        g0.append(jnp.where(low, ch, 0.0))
        g1.append(jnp.where(low, 0.0, ch))
    return jnp.concatenate(g0 + g1, axis=1).astype(BF16)


def _per_group_lanes(x, t):
    a, b = x[:, :t], x[:, t:]
    return jnp.concatenate([a] * NSA_GROUP + [b] * NSA_GROUP, axis=1)


def _select_blocks_t(imp, qpos, n_sel):
    blk = lax.broadcasted_iota(jnp.int32, imp.shape, 0)
    forced = (blk == qpos // SEL_BLOCK) | (blk == 0)
    future = blk * SEL_BLOCK > qpos
    imp = jnp.where(forced, FORCED_SCORE, jnp.where(future, -1.0, imp))
    rank = jnp.zeros(imp.shape, jnp.int32)
    for i in range(n_sel):
        row = imp[i:i + 1, :]
        ahead = (row > imp) | ((row == imp) & (blk > i))
        rank = rank + ahead.astype(jnp.int32)
    return (rank < min(SEL_TOPN, n_sel)).astype(F32)


def _nsa_prompt_kernel(nqct_ref, nqrt_ref, bgt_ref, ng_ref, cmp_ref, ks_ref, vst_ref, wk_ref, wvt_ref,
                       o_ref, m_ref, l_ref, acc_ref, oc_ref, os_ref, ow_ref):
    tq = nqct_ref.shape[1]
    tk = tq
    n_chains = acc_ref.shape[0]
    tc = tq // n_chains
    chains = range(n_chains)
    n_sel = cmp_ref.shape[0]
    i = pl.program_id(1)
    q0 = i * tq
    n_rows = NSA_HEADS * tc
    pos_in = lax.broadcasted_iota(jnp.int32, (1, n_rows), 1) % tc
    pos2_in = lax.broadcasted_iota(jnp.int32, (1, NSA_KV_HEADS * tc), 1) % tc
    krow = lax.broadcasted_iota(jnp.int32, (tk, n_rows), 0)
    blk = lax.broadcasted_iota(jnp.int32, (n_sel, n_rows), 0)

    qrts, q_augs = [None] * n_chains, [None] * n_chains

    def select_body(c):
        cols = slice(c * tc, (c + 1) * tc)
        qct = _stack_nsa_heads_t(nqct_ref, cols, SCALE)
        qrt = _stack_nsa_heads_t(nqrt_ref, cols, SCALE * LOG2E)
        qpos = q0 + c * tc + pos_in
        yield
        s_e = _dot(cmp_ref[:, 0:128].astype(BF16), qct)
        s_o = _dot(cmp_ref[:, 128:256].astype(BF16), qct)
        yield
        s_e = jnp.where(blk * SEL_BLOCK + (CMP_BLOCK - 1) <= qpos, s_e, NEG_INF)
        s_o = jnp.where(blk * SEL_BLOCK + (SEL_BLOCK - 1) <= qpos, s_o, NEG_INF)
        m = jnp.maximum(jnp.max(s_e, axis=0, keepdims=True), jnp.max(s_o, axis=0, keepdims=True))
        m = jnp.where(m == NEG_INF, 0.0, m)
        p_e = jnp.exp(s_e - m)
        p_o = jnp.exp(s_o - m)
        denom = jnp.maximum(jnp.sum(p_e, axis=0, keepdims=True) + jnp.sum(p_o, axis=0, keepdims=True), 1e-30)
        p_e = p_e / denom
        p_o = p_o / denom
        yield
        oc_ref[c] = (_dot_tn(cmp_ref[:, 256:384].astype(BF16), p_e.astype(BF16))
                     + _dot_tn(cmp_ref[:, 384:512].astype(BF16), p_o.astype(BF16)))
        pair = p_e + p_o
        imp = []
        for g in range(NSA_KV_HEADS):
            acc = pair[:, (g * NSA_GROUP) * tc:(g * NSA_GROUP + 1) * tc]
            for h in range(1, NSA_GROUP):
                acc = acc + pair[:, (g * NSA_GROUP + h) * tc:(g * NSA_GROUP + h + 1) * tc]
            imp.append(acc)
        yield
        sel = _select_blocks_t(jnp.concatenate(imp, axis=1), q0 + c * tc + pos2_in, n_sel)
        member_bias = _per_group_lanes((sel - 1.0) * (-MASKED), tc).astype(BF16)
        qrts[c] = qrt
        q_augs[c] = jnp.concatenate([qrt, member_bias], axis=0)

    _side_by_side(select_body, n_chains)

    def reset():
        m_ref[...] = jnp.full_like(m_ref, NEG_INF)
        l_ref[...] = jnp.zeros_like(l_ref)
        acc_ref[...] = jnp.zeros_like(acc_ref)

    def online_softmax_phases(scores, vt):
        m_olds = [m_ref[c] for c in chains]
        m_news = [jnp.maximum(m_old, jnp.max(s, axis=0, keepdims=True)) for m_old, s in zip(m_olds, scores)]
        ps = [jnp.exp2(s - m_new) for s, m_new in zip(scores, m_news)]
        yield
        pvs = [_dot(vt, p.astype(BF16)) for p in ps]
        yield
        for c in chains:
            alpha = jnp.exp2(m_olds[c] - m_news[c])
            l_ref[c] = alpha * l_ref[c] + jnp.sum(ps[c], axis=0, keepdims=True)
            acc_ref[c] = alpha * acc_ref[c] + pvs[c]
            m_ref[c] = m_news[c]

    def online_softmax(scores, vt):
        for _ in online_softmax_phases(scores, vt):
            pass

    def finish(dst_ref):
        for c in chains:
            dst_ref[c] = acc_ref[c] / jnp.maximum(l_ref[c], 1e-30)

    reset()

    def slc_phases(j, diagonal):
        start = pl.multiple_of(j * tk, tk)
        key_blk = (j * tk + lax.broadcasted_iota(jnp.int32, (tk, n_sel), 0)) // SEL_BLOCK
        onehot = (key_blk == lax.broadcasted_iota(jnp.int32, (tk, n_sel), 1)).astype(BF16)
        k_aug = jnp.concatenate([ks_ref[pl.ds(start, tk), :].astype(BF16), onehot], axis=1)
        scores = [_dot(k_aug, q_aug) for q_aug in q_augs]
        if diagonal:
            scores = [jnp.where(krow <= pos_in + c * tc, s, MASKED) for c, s in zip(chains, scores)]
        yield
        yield from online_softmax_phases(scores, vst_ref[0, 0, :, pl.ds(start, tk)].astype(BF16))

    def slc_body(j, carry):
        for _ in slc_phases(j, False):
            pass
        return carry

    def slc_diagonal():
        yield from slc_phases(i, True)
        finish(os_ref)

    lax.fori_loop(0, i, slc_body, 0)

    span = WINDOW + tc

    def window_slab():
        kc = lax.broadcasted_iota(jnp.int32, (span, tc), 0)
        qr = lax.broadcasted_iota(jnp.int32, (span, tc), 1)
        bias = jnp.where((kc > qr) & (kc <= qr + WINDOW), 0.0, MASKED)
        bias = jnp.concatenate([bias] * NSA_HEADS, axis=1)
        starts = [pl.multiple_of(q0 + c * tc - WINDOW, tc) for c in chains]
        scores = [_dot(wk_ref[pl.ds(start, span), :].astype(BF16), qrt) + bias for start, qrt in zip(starts, qrts)]
        yield
        ps = [jnp.exp2(s - jnp.max(s, axis=0, keepdims=True)) for s in scores]
        yield
        pvs = [_dot(wvt_ref[0, :, pl.ds(start, span)].astype(BF16), p.astype(BF16)) for start, p in zip(starts, ps)]
        yield
        for c in chains:
            ow_ref[c] = pvs[c] / jnp.sum(ps[c], axis=0, keepdims=True)

    @pl.when(q0 >= WINDOW)
    def _():
        _side_by_side(lambda which: (slc_diagonal, window_slab)[which](), 2)

    @pl.when(q0 < WINDOW)
    def _():
        for _ in slc_diagonal():
            pass
        reset()

        def win_body(j, carry):
            start = pl.multiple_of(j * tk, tk)
            k = wk_ref[pl.ds(start, tk), :].astype(BF16)
            dists = [(q0 + c * tc + pos_in) - (krow + j * tk) for c in chains]
            scores = [jnp.where((dist >= 0) & (dist < WINDOW), _dot(k, qrt), MASKED) for dist, qrt in zip(dists, qrts)]
            online_softmax(scores, wvt_ref[0, :, pl.ds(start, tk)].astype(BF16))
            return carry

        lax.fori_loop(0, i + 1, win_body, 0)
        finish(ow_ref)

    low = lax.broadcasted_iota(jnp.int32, (LANES, tc), 0) < HEAD_DIM
    for c in chains:
        rows = slice(c * tc, (c + 1) * tc)
        gate = 1.0 / (1.0 + jnp.exp(-bgt_ref[:, rows]))
        o_c, o_s, o_w = oc_ref[c], os_ref[c], ow_ref[c]
        for ch in range(NSA_GROUP):
            per_group = []
            for g in range(NSA_KV_HEADS):
                hd = g * NSA_GROUP + ch
                sl = slice(hd * tc, (hd + 1) * tc)
                per_group.append(gate[3 * hd:3 * hd + 1, :] * o_c[:, sl] + gate[3 * hd + 1:3 * hd + 2, :] * o_s[:, sl]
                                 + gate[3 * hd + 2:3 * hd + 3, :] * o_w[:, sl])
            mixed = jnp.where(low, per_group[0], per_group[1])
            cs = slice(ch * LANES, (ch + 1) * LANES)
            o_ref[rows, cs] = mixed.T * _silu(ng_ref[rows, cs])


def _nsa_prompt(nqct, nqrt, bgt, ngate, cmp, nsa_new, nsa_all, win_new, win_t, layer, batch, seq):
    tq = K_TILE
    n_chains = K_TILE // Q_TILE
    nq = seq // tq
    n_sel = seq // SEL_BLOCK
    qcol = lambda f: pl.BlockSpec((f, tq), lambda b, i: (0, b * nq + i))
    qrow = lambda w: pl.BlockSpec((tq, w), lambda b, i: (b * nq + i, 0))
    n_rows = NSA_HEADS * Q_TILE
    stat = pltpu.VMEM((n_chains, 1, n_rows), F32)
    wide = pltpu.VMEM((n_chains, LANES, n_rows), F32)
    return pl.pallas_call(
        _nsa_prompt_kernel,
        grid=(batch, nq),
        in_specs=[qcol(512), qcol(512), qcol(128), qrow(512),
                  pl.BlockSpec((n_sel, 512), lambda b, i: (b, 0)),
                  pl.BlockSpec((seq, LANES), lambda b, i: (b, 2)),
                  pl.BlockSpec((1, 1, LANES, seq), lambda b, i: (layer, b, 3, 0)),
                  pl.BlockSpec((seq, LANES), lambda b, i: (b, 0)),
                  pl.BlockSpec((1, LANES, seq), lambda b, i: (b, 1, 0))],
        out_specs=qrow(512),
        out_shape=jax.ShapeDtypeStruct((batch * seq, 512), F32),
        scratch_shapes=[stat, stat, wide, wide, wide, wide],
        compiler_params=pltpu.CompilerParams(dimension_semantics=("parallel", "parallel"),
                                             vmem_limit_bytes=VMEM_LIMIT),
        name="nsa_prompt",
    )(nqct, nqrt, bgt, ngate, cmp, nsa_new, nsa_all, win_new, win_t)


def _stack_nsa_heads(ref, rows):
    low = lax.broadcasted_iota(jnp.int32, (rows.stop - rows.start, LANES), 1) < HEAD_DIM
    g0, g1 = [], []
    for c in range(NSA_GROUP):
        ch = ref[rows, c * LANES:(c + 1) * LANES] * SCALE
        g0.append(jnp.where(low, ch, 0.0))
        g1.append(jnp.where(low, 0.0, ch))
    return jnp.concatenate(g0 + g1, axis=0).astype(BF16)


def _per_group_rows(x, t):
    a, b = x[:t], x[t:]
    return jnp.concatenate([a] * NSA_GROUP + [b] * NSA_GROUP, axis=0)


def _select_blocks(imp, qpos, n_sel):
    blk = lax.broadcasted_iota(jnp.int32, imp.shape, 1)
    forced = (blk == qpos // SEL_BLOCK) | (blk == 0)
    future = blk * SEL_BLOCK > qpos
    imp = jnp.where(forced, FORCED_SCORE, jnp.where(future, -1.0, imp))
    rank = jnp.zeros(imp.shape, jnp.int32)
    for i in range(n_sel):
        col = imp[:, i:i + 1]
        ahead = (col > imp) | ((col == imp) & (blk > i))
        rank = rank + ahead.astype(jnp.int32)
    return ((rank < min(SEL_TOPN, n_sel)) & (blk < n_sel)).astype(BF16)


def _expand_matrix(n_rows, n_keys, key0):
    r = lax.broadcasted_iota(jnp.int32, (n_rows, n_keys), 0)
    c = lax.broadcasted_iota(jnp.int32, (n_rows, n_keys), 1)
    return ((key0 + c) // SEL_BLOCK == r).astype(BF16)


def _softmax_parts(parts):
    m = None
    for s, _, _ in parts:
        pm = jnp.max(s, axis=1, keepdims=True)
        m = pm if m is None else jnp.maximum(m, pm)
    m = jnp.where(m == NEG_INF, 0.0, m)
    denom = None
    acc = None
    for s, v, feature_major in parts:
        p = jnp.exp(s - m)
        ps = jnp.sum(p, axis=1, keepdims=True)
        pv = _dot_nt(p.astype(BF16), v) if feature_major else _dot(p.astype(BF16), v)
        denom = ps if denom is None else denom + ps
        acc = pv if acc is None else acc + pv
    return acc / jnp.maximum(denom, 1e-30)


def _nsa_sample_kernel(pt_ref, nqc_ref, nqr_ref, new_ref, winn_ref, bg_ref, ng_ref, swin_ref, phi_ref, seg_ref,
                       *rest, n_pages, page, pos0, group):
    all_pages = rest[:n_pages * group]
    o_ref, ckall_ref, cvall_ref, newpad_ref, winpad_ref = rest[n_pages * group:]
    t = nqc_ref.shape[0] // group
    rows = NSA_HEADS * t
    n_sel = (pos0 + t + SEL_BLOCK - 1) // SEL_BLOCK
    wlen = swin_ref.shape[3]
    trow = lax.broadcasted_iota(jnp.int32, (rows, 1), 0) % t
    trow2 = lax.broadcasted_iota(jnp.int32, (NSA_KV_HEADS * t, 1), 0) % t
    qpos, qpos2 = pos0 + trow, pos0 + trow2
    col = lax.broadcasted_iota(jnp.int32, (rows, LANES), 1)
    block_end = jnp.where(col < SEL_BLOCK, col * SEL_BLOCK + (CMP_BLOCK - 1),
                          (col - SEL_BLOCK) * SEL_BLOCK + (SEL_BLOCK - 1))
    scol = lax.broadcasted_iota(jnp.int32, (NSA_KV_HEADS * t, page), 1)
    rcol = lax.broadcasted_iota(jnp.int32, (NSA_KV_HEADS * t, wlen), 1)
    dist = wlen + trow2 - rcol
    bias_w = _per_group_rows(jnp.where((dist >= 0) & (dist < WINDOW), 0.0, NEG_INF), t)
    dist_n = trow2 - scol
    bias_wn = _per_group_rows(jnp.where((dist_n >= 0) & (dist_n < WINDOW) & (scol < t), 0.0, NEG_INF), t)
    low = lax.broadcasted_iota(jnp.int32, (t, LANES), 1) < HEAD_DIM

    def body(k):
        rs = slice(k * t, (k + 1) * t)
        pages = all_pages[k * n_pages:(k + 1) * n_pages]
        qc = _stack_nsa_heads(nqc_ref, rs)
        qr = _stack_nsa_heads(nqr_ref, rs)
        for p in range(n_pages):
            ckall_ref[k, :, p * page:(p + 1) * page] = (pages[p][0, 0, 0:128, :] * phi_ref[0:1, :]).astype(BF16)
            cvall_ref[k, :, p * page:(p + 1) * page] = (pages[p][0, 0, 128:256, :] * phi_ref[1:2, :]).astype(BF16)
        newpad_ref[k] = jnp.zeros(newpad_ref.shape[1:], F32)
        newpad_ref[k, 0:t, :] = new_ref[rs, :]
        winpad_ref[k] = jnp.zeros(winpad_ref.shape[1:], F32)
        winpad_ref[k, 0:t, :] = winn_ref[rs, :]
        yield
        kcmp_t = _dot(ckall_ref[k], seg_ref[...]).astype(BF16)
        vcmp_t = _dot(cvall_ref[k], seg_ref[...]).astype(BF16)
        yield
        s = jnp.where(block_end <= qpos, _dot(qc, kcmp_t), NEG_INF)
        m = jnp.max(s, axis=1, keepdims=True)
        m = jnp.where(m == NEG_INF, 0.0, m)
        p_c = jnp.exp(s - m)
        p_c = p_c / jnp.maximum(jnp.sum(p_c, axis=1, keepdims=True), 1e-30)
        yield
        o_c = _dot_nt(p_c.astype(BF16), vcmp_t)
        imp = []
        for g in range(NSA_KV_HEADS):
            acc = p_c[(g * NSA_GROUP) * t:(g * NSA_GROUP + 1) * t]
            for c in range(1, NSA_GROUP):
                acc = acc + p_c[(g * NSA_GROUP + c) * t:(g * NSA_GROUP + c + 1) * t]
            imp.append(acc)
        imp = jnp.concatenate(imp, axis=0)
        imp = imp + pltpu.roll(imp, SEL_BLOCK, 1)
        sel = _select_blocks(imp, qpos2, n_sel)
        yield
        member = _dot(sel, _expand_matrix(LANES, n_pages * page, 0))
        member_new = _dot(sel, _expand_matrix(LANES, page, n_pages * page))
        raw = [_dot(qr, pages[p][0, 0, 256:384, :].astype(BF16)) for p in range(n_pages)]
        raw_new = _dot_nt(qr, newpad_ref[k, :, 256:384].astype(BF16))
        raw_w = _dot(qr, swin_ref[0, k, 0:128, :].astype(BF16))
        raw_wn = _dot_nt(qr, winpad_ref[k, :, 0:128].astype(BF16))
        yield
        bias_past = _per_group_rows(jnp.where(member > 0.5, 0.0, NEG_INF), t)
        ok_new = (member_new > 0.5) & (scol <= trow2) & (scol < t)
        bias_new = _per_group_rows(jnp.where(ok_new, 0.0, NEG_INF), t)
        parts = [(raw[p] + bias_past[:, p * page:(p + 1) * page], pages[p][0, 0, 384:512, :].astype(BF16), True)
                 for p in range(n_pages)]
        parts.append((raw_new + bias_new, newpad_ref[k, :, 384:512].astype(BF16), False))
        o_s = _softmax_parts(parts)
        yield
        o_w = _softmax_parts([(raw_w + bias_w, swin_ref[0, k, 128:256, :].astype(BF16), True),
                              (raw_wn + bias_wn, winpad_ref[k, :, 128:256].astype(BF16), False)])
        yield
        gate = 1.0 / (1.0 + jnp.exp(-bg_ref[rs, :]))
        for c in range(NSA_GROUP):
            per_group = []
            for g in range(NSA_KV_HEADS):
                r0 = (g * NSA_GROUP + c) * t
                cg = (g * NSA_GROUP + c) * 3
                per_group.append(gate[:, cg:cg + 1] * o_c[r0:r0 + t] + gate[:, cg + 1:cg + 2] * o_s[r0:r0 + t]
                                 + gate[:, cg + 2:cg + 3] * o_w[r0:r0 + t])
            mixed = jnp.where(low, per_group[0], per_group[1])
            sl = slice(c * LANES, (c + 1) * LANES)
            o_ref[rs, sl] = mixed * _silu(ng_ref[rs, sl])

    _side_by_side(body, group)


def _nsa_sample(page_table, nqc, nqr, nsa_new, win_new, bgate, ngate, state_win_t, phi_rows, seg, cache_t,
                layer, row0, dec_batch, t, pos0):
    n_pages = page_table.shape[1]
    page = cache_t.shape[3]
    group = SAMPLE_GROUP
    gt = group * t
    blk0 = row0 // gt
    own = lambda w: pl.BlockSpec((gt, w), lambda b, pt: (b, 0))
    joint = lambda w: pl.BlockSpec((gt, w), lambda b, pt: (blk0 + b, 0))
    wlen = state_win_t.shape[3]
    n_keys = n_pages * page
    grid_spec = pltpu.PrefetchScalarGridSpec(
        num_scalar_prefetch=1,
        grid=(dec_batch // group,),
        in_specs=[own(512), own(512), joint(512), joint(256), own(128), joint(512),
                  pl.BlockSpec((1, group, 256, wlen), lambda b, pt: (layer, b, 0, 0)),
                  pl.BlockSpec((2, page), lambda b, pt: (0, 0)),
                  pl.BlockSpec((n_keys, LANES), lambda b, pt: (0, 0))]
        + _page_specs(layer, n_pages, page, group),
        out_specs=pl.BlockSpec((gt, 512), lambda b, pt: (b, 0)),
        scratch_shapes=[pltpu.VMEM((group, LANES, n_keys), BF16), pltpu.VMEM((group, LANES, n_keys), BF16),
                        pltpu.VMEM((group, page, 512), F32), pltpu.VMEM((group, page, 256), F32)],
    )
    return pl.pallas_call(
        functools.partial(_nsa_sample_kernel, n_pages=n_pages, page=page, pos0=pos0, group=group),
        grid_spec=grid_spec,
        out_shape=jax.ShapeDtypeStruct((dec_batch * t, 512), F32),
        compiler_params=pltpu.CompilerParams(dimension_semantics=("parallel",), vmem_limit_bytes=VMEM_LIMIT),
        name="nsa_sample",
    )(page_table, nqc, nqr, nsa_new, win_new, bgate, ngate, state_win_t, phi_rows, seg,
      *([cache_t] * (n_pages * group)))


def _out_proj_kernel(oap_ref, obp_ref, ocp_ref, oas_ref, obs_ref, ocs_ref, x_ref, w_ref, g_ref, y_ref, *,
                     n_prompt_tiles):
    prompt = pl.program_id(0) < n_prompt_tiles
    pick = lambda p_ref, s_ref: jnp.where(prompt, p_ref[...], s_ref[...]).astype(BF16)
    out = (_dot(pick(oap_ref, oas_ref), w_ref[0:256, :])
           + _dot(pick(obp_ref, obs_ref), w_ref[256:512, :])
           + _dot(pick(ocp_ref, ocs_ref), w_ref[512:1024, :]))
    ms = jnp.mean(out * out, axis=-1, keepdims=True)
    y_ref[...] = x_ref[...] + out * lax.rsqrt(ms + NORM_EPS) * g_ref[...]


def _out_proj(prompt_parts, sample_parts, x, w_out_p, g_post, n_p):
    n = x.shape[0]
    tm = ROW_TILE
    npt = n_p // tm
    row = lambda w: pl.BlockSpec((tm, w), lambda i: (i, 0))
    prompt_row = lambda w: pl.BlockSpec((tm, w), lambda i: (jnp.minimum(i, npt - 1), 0))
    sample_row = lambda w: pl.BlockSpec((tm, w), lambda i: (jnp.maximum(i - npt, 0), 0))
    fixed = lambda s: pl.BlockSpec(s, lambda i: (0, 0), pipeline_mode=pl.Buffered(1))
    widths = (256, 256, 512)
    return pl.pallas_call(
        functools.partial(_out_proj_kernel, n_prompt_tiles=npt),
        grid=(n // tm,),
        in_specs=[prompt_row(w) for w in widths] + [sample_row(w) for w in widths]
        + [row(D_MODEL), fixed((D_MODEL, D_MODEL)), fixed((1, D_MODEL))],
        out_specs=row(D_MODEL),
        out_shape=jax.ShapeDtypeStruct((n, D_MODEL), F32),
        compiler_params=pltpu.CompilerParams(dimension_semantics=("arbitrary",), vmem_limit_bytes=VMEM_LIMIT),
        name="out_proj",
    )(*prompt_parts, *sample_parts, x, w_out_p, g_post)


def _win_update_kernel(state_ref, new_ref, o_ref, *, nb, t):
    wlen = state_ref.shape[3]
    for bi in range(nb):
        o_ref[0, bi] = pltpu.roll(state_ref[0, bi], wlen - t, 1)
        o_ref[0, bi, :, wlen - t:wlen] = new_ref[0, bi * t:(bi + 1) * t, :].T


def _win_update(state_win_t, win_rows, t):
    depth, bs, width, wlen = state_win_t.shape
    nb = 8
    return pl.pallas_call(
        functools.partial(_win_update_kernel, nb=nb, t=t),
        grid=(depth, bs // nb),
        in_specs=[pl.BlockSpec((1, nb, width, wlen), lambda l, b: (l, b, 0, 0)),
                  pl.BlockSpec((1, nb * t, width), lambda l, b: (l, b, 0))],
        out_specs=pl.BlockSpec((1, nb, width, wlen), lambda l, b: (l, b, 0, 0)),
        out_shape=jax.ShapeDtypeStruct(state_win_t.shape, F32),
        compiler_params=pltpu.CompilerParams(dimension_semantics=("parallel", "parallel"),
                                             vmem_limit_bytes=VMEM_LIMIT),
        name="win_update",
    )(state_win_t, win_rows)


def _to_chunk_layout(w, axis):
    shape = w.shape
    split = shape[:axis] + (NSA_KV_HEADS, NSA_GROUP, HEAD_DIM) + shape[axis + 1:]
    return jnp.swapaxes(w.reshape(split), axis, axis + 1).reshape(shape)


def _rope_tables(pos):
    inv = ROPE_THETA ** (-jnp.arange(ROT_HALF, dtype=F32) * (2.0 / ROT_DIM))
    ang = pos.astype(F32)[:, None] * inv[None, :]
    cos, sin = jnp.cos(ang), jnp.sin(ang)
    n = pos.shape[0]
    cos_h = jnp.concatenate([cos, cos, jnp.ones((n, HEAD_DIM - ROT_DIM), F32)], axis=1)
    sin_h = jnp.concatenate([-sin, sin, jnp.zeros((n, HEAD_DIM - ROT_DIM), F32)], axis=1)
    reps = (1, LANES // HEAD_DIM)
    return jnp.tile(cos_h, reps), jnp.tile(sin_h, reps), cos.T, sin.T


def _segment_matrix(n_keys):
    n = np.arange(n_keys) // CMP_BLOCK
    col = (n % 2) * SEL_BLOCK + n // 2
    return jnp.asarray(col[:, None] == np.arange(LANES)[None, :], BF16)


def kernel(x_prompt, x_sample, cache_sb_kv, cache_nsa_kv, state_win_kv, state_pool, page_table, norm_pre, w_in,
           pool_w, pool_scale, phi_k, phi_v, w_out, norm_post):
    bp, tp, d = x_prompt.shape
    bs, ts, _ = x_sample.shape
    depth = w_in.shape[0]
    n_pool, page = cache_sb_kv.shape[1], cache_sb_kv.shape[2]
    past_len = page_table.shape[1] * page
    wlen = state_win_kv.shape[2]
    n_p, n_s = bp * tp, bs * ts
    assert d == D_MODEL and n_p % ROW_TILE == 0 and n_s % ROW_TILE == 0
    assert tp % K_TILE == 0 and K_TILE % Q_TILE == 0 and tp % SEL_BLOCK == 0 and n_p % n_s == 0
    assert past_len % SEL_BLOCK == 0 and ts < CMP_BLOCK and ts % 8 == 0 and page == LANES
    assert past_len // CMP_BLOCK <= SEL_BLOCK and (wlen >= WINDOW or wlen == past_len)

    o = np.cumsum((0,) + (256,) * 6 + (512,) + (128,) * 6 + (24, 512)).tolist()
    w_rows = jnp.swapaxes(w_in, 1, 2)
    w_t = jnp.concatenate([
        w_rows[:, o[0]:o[1]], _to_chunk_layout(w_rows[:, o[6]:o[7]], 1), w_rows[:, o[13]:o[14]],
        jnp.zeros((depth, IN_WIDTH_P - o[15], D_MODEL), w_in.dtype),
        w_rows[:, o[1]:o[6]], w_rows[:, o[7]:o[13]], _to_chunk_layout(w_rows[:, o[14]:o[15]], 1)],
        axis=1).astype(BF16)
    w_out_p = jnp.concatenate([w_out[:, :512], _to_chunk_layout(w_out[:, 512:], 1)], axis=1).astype(BF16)
    w_pool = jnp.zeros((depth, POOL_WIDTH, POOL_WIDTH), F32)
    for g in range(len(POOL_WINDOWS)):
        sl = slice(g * POOL_GDIM, (g + 1) * POOL_GDIM)
        w_pool = w_pool.at[:, sl, sl].set(pool_w[:, g])
    w_pool = w_pool.astype(BF16)
    reps = page // CMP_BLOCK
    phi_rows = jnp.stack([jnp.tile(phi_k, (1, reps)), jnp.tile(phi_v, (1, reps))], axis=1)
    seg = _segment_matrix(past_len)

    pos = jnp.concatenate([jnp.tile(jnp.arange(tp, dtype=jnp.int32), bp),
                           jnp.tile(past_len + jnp.arange(ts, dtype=jnp.int32), bs)])
    tables = _rope_tables(pos)

    cache_sb_t = jnp.transpose(cache_sb_kv, (0, 1, 3, 4, 5, 2)).reshape(depth, n_pool, 512, page)
    cache_nsa_t = jnp.transpose(cache_nsa_kv, (0, 1, 3, 4, 5, 2)).reshape(depth, n_pool, 512, page)
    state_win_t = jnp.transpose(state_win_kv, (0, 1, 3, 4, 5, 2)).reshape(depth, bs, 256, wlen)

    x = jnp.concatenate([x_prompt.reshape(n_p, d), x_sample.reshape(n_s, d)], axis=0)
    sbkv_all = jnp.zeros((depth, bp, 512, tp), F32)
    nsa_all = jnp.zeros((depth, bp, 512, tp), F32)
    keep_p = min(WINDOW, tp)
    win_p, pool_p, sb_s, nsa_s, win_rows, pool_s = [], [], [], [], [], []
    for layer in range(depth):
        (sbkv, sbg, pu, pg, nsa_new, win_new, ngate, cmp, sbq, nqc, nqr, bgate,
         sbqt, nqct, nqrt, bgt, sbkv_all, nsa_all, win_t) = _in_proj(
            x, norm_pre[layer][None], w_t[layer], phi_k[layer], phi_v[layer], tables, sbkv_all, nsa_all,
            layer, n_p, bp, tp)
        pool_args = (pu, pg, w_pool[layer], pool_scale[layer][None])
        prompt_parts = (
            _sb_prompt(sbqt, sbkv, sbkv_all, sbg, layer, bp, tp),
            _pool(*pool_args, None, layer, 0, 1, tp, bp, 0),
            _nsa_prompt(nqct, nqrt, bgt, ngate, cmp, nsa_new, nsa_all, win_new, win_t, layer, bp, tp))
        sample_parts = (
            _sb_sample(page_table, sbq, sbkv, sbg, cache_sb_t, layer, n_p, bs, ts),
            _pool(*pool_args, state_pool, layer, n_p, bs, ts, bs, past_len),
            _nsa_sample(page_table, nqc, nqr, nsa_new, win_new, bgate, ngate, state_win_t, phi_rows[layer], seg,
                        cache_nsa_t, layer, n_p, bs, ts, past_len))
        x = _out_proj(prompt_parts, sample_parts, x, w_out_p[layer], norm_post[layer][None], n_p)

        win_p.append(win_t[:, :, tp - keep_p:])
        pool_p.append(pu[:n_p].reshape(bp, tp, POOL_WIDTH)[:, tp - POOL_BUF:])
        sb_s.append(sbkv[n_p:].reshape(bs, ts, 2, SB_HEADS, HEAD_DIM))
        nsa_s.append(nsa_new[n_p:].reshape(bs, ts, 4, NSA_KV_HEADS, HEAD_DIM))
        win_rows.append(win_new[n_p:])
        pool_s.append(jnp.concatenate([state_pool[layer], pu[n_p:].reshape(bs, ts, 256)], axis=1)[:, -POOL_BUF:])
    win_s_t = _win_update(state_win_t, jnp.stack(win_rows), ts)

    def token_major(a, n_kv, n_heads):
        lead = a.shape[:-2]
        a = a.reshape(lead + (n_kv, n_heads, HEAD_DIM, a.shape[-1]))
        return jnp.moveaxis(a, -1, len(lead))

    y_prompt = x[:n_p].reshape(bp, tp, d)
    y_sample = x[n_p:].reshape(bs, ts, d)
    return (y_prompt, y_sample,
            token_major(sbkv_all, 2, SB_HEADS), token_major(nsa_all, 4, NSA_KV_HEADS),
            token_major(jnp.stack(win_p), 2, NSA_KV_HEADS), jnp.stack(pool_p),
            jnp.stack(sb_s), jnp.stack(nsa_s), token_major(win_s_t, 2, NSA_KV_HEADS), jnp.stack(pool_s))
```

```python
import functools

import jax
import jax.numpy as jnp
import numpy as np
from jax import lax
from jax.experimental import pallas as pl
from jax.experimental.pallas import tpu as pltpu

F32 = jnp.float32
BF16 = jnp.bfloat16

D_MODEL = 1024
HEAD_DIM = 64
SB_WIDTH = 256
SB_HEADS = 4
POOL_WINDOWS = (2, 4, 8, 16)
POOL_WIDTH = 256
POOL_GDIM = 64
POOL_BUF = 15
NSA_WIDTH = 512
NSA_HEADS = 8
NSA_KV_HEADS = 2
NSA_GROUP = 4
NSA_KV_WIDTH = 128
CMP_BLOCK = 32
SEL_BLOCK = 64
SEL_TOPN = 8
WINDOW = 512
ROPE_THETA = 500000.0
ROT_DIM = 16
ROT_HALF = ROT_DIM // 2
NORM_EPS = 1e-6
FORCED_SCORE = 1e4
SCALE = HEAD_DIM ** -0.5
LOG2E = 1.4426950408889634
NEG_INF = float("-inf")
MASKED = -(2.0 ** 100)

C_SBQ, C_NQ, C_BG, C_QEND = 0, 256, 768, 896
C_SBK, C_SBG, C_PU, C_PG = 896, 1408, 1664, 1920
C_NSA = 2176
C_WIN = 2688
C_NG = 2944
IN_WIDTH_P = 3456
LANES = 128
ROW_TILE = 512
Q_TILE = 128
K_TILE = 256
NSA_TILE = 256
SAMPLE_GROUP = 4
VMEM_LIMIT = 56 * 1024 * 1024


def _dot(a, b):
    return jnp.dot(a, b, preferred_element_type=F32)


def _dot_nt(a, b):
    return lax.dot_general(a, b, (((1,), (1,)), ((), ())), preferred_element_type=F32)


def _dot_tn(a, b):
    return lax.dot_general(a, b, (((0,), (0,)), ((), ())), preferred_element_type=F32)


def _silu(g):
    return g / (1.0 + jnp.exp(-g))


def _softplus(z):
    return jnp.maximum(z, 0.0) + jnp.log(1.0 + jnp.exp(-jnp.abs(z)))


def _split_bf16(x):
    hi = x.astype(BF16)
    return hi, (x - hi.astype(F32)).astype(BF16)


def _suffix_sum_exclusive(x, u_bf16):
    hi, lo = _split_bf16(x)
    n = x.shape[0]
    both = _dot(jnp.concatenate([hi, lo], axis=0), u_bf16)
    return both[:n] + both[n:]


def _later_matrix(n, transposed=False):
    r = lax.broadcasted_iota(jnp.int32, (n, n), 0)
    c = lax.broadcasted_iota(jnp.int32, (n, n), 1)
    return ((c > r) if transposed else (r > c)).astype(BF16)


def _in_proj_kernel(phik_ref, phiv_ref, x_ref, g_ref, wt_ref, cos_ref, sin_ref, cost_ref, sint_ref,
                    sbkv_prev_ref, nsa_prev_ref,
                    sbkv_ref, sbg_ref, pu_ref, pg_ref, nsa_ref, win_ref, ng_ref, cmp_ref,
                    sbq_ref, nqc_ref, nqr_ref, bg_ref,
                    sbqt_ref, nqct_ref, nqrt_ref, bgt_ref, sbkvt_ref, nsat_ref, wint_ref,
                    ck_ref, cv_ref, *, n_prompt_tiles):
    del sbkv_prev_ref, nsa_prev_ref
    i = pl.program_id(0)
    x = x_ref[...]
    ms = jnp.mean(x * x, axis=-1, keepdims=True)
    h = (x * lax.rsqrt(ms + NORM_EPS) * g_ref[...]).astype(BF16)

    def proj(a, b):
        return _dot_nt(h, wt_ref[a:b, :])

    cos = cos_ref[...]
    sin = sin_ref[...]
    first = (lax.broadcasted_iota(jnp.int32, cos.shape, 1) % HEAD_DIM) < ROT_HALF

    def rope(v):
        swapped = jnp.where(first, pltpu.roll(v, LANES - ROT_HALF, 1), pltpu.roll(v, ROT_HALF, 1))
        return v * cos + swapped * sin

    sbkv_ref[...] = proj(C_SBK, C_SBK + 512)
    gates = proj(C_SBG, C_SBG + 768)
    sbg_ref[...] = gates[:, 0:256]
    pu_ref[...] = gates[:, 256:512]
    pg_ref[...] = gates[:, 512:768]
    ng_ref[...] = proj(C_NG, C_NG + 512)
    nsa = proj(C_NSA, C_NSA + 512)
    ck_ref[...] = nsa[:, 0:128]
    cv_ref[...] = nsa[:, 128:256]
    nsa_ref[:, 0:256] = nsa[:, 0:256]
    nsa_ref[:, 256:384] = rope(nsa[:, 256:384])
    nsa_ref[:, 384:512] = nsa[:, 384:512]
    win = proj(C_WIN, C_WIN + 256)
    win_ref[:, 0:128] = rope(win[:, 0:128])
    win_ref[:, 128:256] = win[:, 128:256]

    nb = cmp_ref.shape[0]
    for part, (src_ref, phi_ref) in enumerate(((ck_ref, phik_ref), (cv_ref, phiv_ref))):
        acc_e = jnp.zeros((nb, LANES), F32)
        acc_o = jnp.zeros((nb, LANES), F32)
        for l in range(CMP_BLOCK):
            acc_e = acc_e + src_ref[pl.ds(l, nb, stride=SEL_BLOCK), :] * phi_ref[l]
            acc_o = acc_o + src_ref[pl.ds(CMP_BLOCK + l, nb, stride=SEL_BLOCK), :] * phi_ref[l]
        cmp_ref[:, part * 256:part * 256 + 128] = acc_e
        cmp_ref[:, part * 256 + 128:part * 256 + 256] = acc_o

    @pl.when(i >= n_prompt_tiles)
    def _():
        q_all = proj(C_SBQ, C_QEND)
        sbq_ref[...] = q_all[:, C_SBQ:C_SBQ + 256]
        bg_ref[...] = q_all[:, C_BG:C_BG + 128]
        nqc_ref[...] = q_all[:, C_NQ:C_NQ + 512]
        for c in range(NSA_GROUP):
            nqr_ref[:, c * LANES:(c + 1) * LANES] = rope(q_all[:, C_NQ + c * LANES:C_NQ + (c + 1) * LANES])

    @pl.when(i < n_prompt_tiles)
    def _():
        cos_t = cost_ref[...]
        sin_t = sint_ref[...]

        def rope_t(v):
            pieces = []
            for base in range(0, LANES, HEAD_DIM):
                x1 = v[base:base + ROT_HALF]
                x2 = v[base + ROT_HALF:base + ROT_DIM]
                pieces += [x1 * cos_t - x2 * sin_t, x2 * cos_t + x1 * sin_t, v[base + ROT_DIM:base + HEAD_DIM]]
            return jnp.concatenate(pieces, axis=0)

        qt_all = _dot_nt(wt_ref[C_SBQ:C_QEND, :], h)
        sbqt_ref[...] = qt_all[C_SBQ:C_SBQ + 256]
        bgt_ref[...] = qt_all[C_BG:C_BG + 128]
        nqct_ref[...] = qt_all[C_NQ:C_NQ + 512]
        for c in range(NSA_GROUP):
            nqrt_ref[c * LANES:(c + 1) * LANES, :] = rope_t(qt_all[C_NQ + c * LANES:C_NQ + (c + 1) * LANES])
        sbkvt_ref[0, 0] = sbkv_ref[...].T
        nsat_ref[0, 0] = nsa_ref[...].T
        wint_ref[0] = win_ref[...].T


def _in_proj(x, g_pre, w_t, phi_k, phi_v, tables, sbkv_all, nsa_all, layer, n_p, batch, seq):
    n = x.shape[0]
    n_s = n - n_p
    tm = ROW_TILE
    npt = n_p // tm
    per_seq = seq // tm
    cos_r, sin_r, cos_t, sin_t = tables
    row = lambda w: pl.BlockSpec((tm, w), lambda i: (i, 0))
    fixed = lambda s: pl.BlockSpec(s, lambda i: (0, 0), pipeline_mode=pl.Buffered(1))
    smem = pl.BlockSpec(memory_space=pltpu.SMEM)
    anywhere = pl.BlockSpec(memory_space=pl.ANY)
    prompt_tile = lambda i: jnp.minimum(i, npt - 1)
    sample_row = lambda w: pl.BlockSpec((tm, w), lambda i: (jnp.maximum(i - npt, 0), 0))
    prompt_col = lambda f: pl.BlockSpec((f, tm), lambda i: (0, prompt_tile(i)))
    stacked = pl.BlockSpec((1, 1, 512, tm), lambda i: (layer, prompt_tile(i) // per_seq, 0, prompt_tile(i) % per_seq))
    per_batch = pl.BlockSpec((1, 256, tm), lambda i: (prompt_tile(i) // per_seq, 0, prompt_tile(i) % per_seq))
    all_widths = (512, 256, 256, 256, 512, 256, 512)
    sample_widths = (256, 512, 512, 128)
    prompt_feats = (256, 512, 512, 128)
    out_shape = ([jax.ShapeDtypeStruct((n, w), F32) for w in all_widths]
                 + [jax.ShapeDtypeStruct((n // SEL_BLOCK, 512), F32)]
                 + [jax.ShapeDtypeStruct((n_s, w), F32) for w in sample_widths]
                 + [jax.ShapeDtypeStruct((f, n_p), F32) for f in prompt_feats]
                 + [jax.ShapeDtypeStruct(sbkv_all.shape, F32), jax.ShapeDtypeStruct(nsa_all.shape, F32),
                    jax.ShapeDtypeStruct((batch, 256, seq), F32)])
    out_specs = ([row(w) for w in all_widths] + [pl.BlockSpec((tm // SEL_BLOCK, 512), lambda i: (i, 0))]
                 + [sample_row(w) for w in sample_widths] + [prompt_col(f) for f in prompt_feats]
                 + [stacked, stacked, per_batch])
    n_fixed_outs = len(all_widths) + 1 + len(sample_widths) + len(prompt_feats)
    return pl.pallas_call(
        functools.partial(_in_proj_kernel, n_prompt_tiles=npt),
        grid=(n // tm,),
        in_specs=[smem, smem, row(D_MODEL), fixed((1, D_MODEL)), fixed((IN_WIDTH_P, D_MODEL)), row(LANES), row(LANES),
                  pl.BlockSpec((ROT_HALF, tm), lambda i: (0, i)), pl.BlockSpec((ROT_HALF, tm), lambda i: (0, i)),
                  anywhere, anywhere],
        out_specs=out_specs,
        out_shape=out_shape,
        input_output_aliases={9: n_fixed_outs, 10: n_fixed_outs + 1},
        scratch_shapes=[pltpu.VMEM((tm, LANES), F32), pltpu.VMEM((tm, LANES), F32)],
        compiler_params=pltpu.CompilerParams(dimension_semantics=("arbitrary",), vmem_limit_bytes=VMEM_LIMIT),
        name="in_proj",
    )(phi_k, phi_v, x, g_pre, w_t, cos_r, sin_r, cos_t, sin_t, sbkv_all, nsa_all)


def _sb_prompt_kernel(qt_ref, k_ref, vt_ref, g_ref, o_ref, acc_ref, run_ref):
    tq = qt_ref.shape[1]
    tk = tq
    i = pl.program_id(1)
    n_chains = acc_ref.shape[0]
    tc = tq // n_chains
    n_rows = SB_HEADS * tc
    qt = qt_ref[...] * SCALE
    head = lax.broadcasted_iota(jnp.int32, (SB_WIDTH, tc), 0) // HEAD_DIM
    q4t = [jnp.concatenate([jnp.where(head == h, qt[:, c * tc:(c + 1) * tc], 0.0) for h in range(SB_HEADS)],
                           axis=1).astype(BF16) for c in range(n_chains)]
    u_t = _later_matrix(tk, transposed=True)
    qrel = lax.broadcasted_iota(jnp.int32, (tk, n_rows), 1) % tc
    krow = lax.broadcasted_iota(jnp.int32, (tk, n_rows), 0)
    acc_ref[...] = jnp.zeros_like(acc_ref)
    run_ref[...] = jnp.zeros_like(run_ref)

    def step(tiles, diagonal):
        chains = range(n_chains)
        starts = [pl.multiple_of(j * tk, tk) for j in tiles]
        ks = [k_ref[pl.ds(start, tk), :].astype(BF16) for start in starts]
        vts = [vt_ref[0, 0, :, pl.ds(start, tk)].astype(BF16) for start in starts]
        zs = [[_dot(k, q4t[c]) for c in chains] for k in ks]
        log_keeps = [[-_softplus(z) for z in per_tile] for per_tile in zs]
        if diagonal:
            masks = [krow < qrel + c * tc for c in chains]
            log_keeps = [[jnp.where(m, lk, 0.0) for m, lk in zip(masks, per_tile)] for per_tile in log_keeps]
        boths = [[_dot(u_t, jnp.concatenate(_split_bf16(lk), axis=1)) for lk in per_tile] for per_tile in log_keeps]
        runs = [run_ref[c] for c in chains]
        totals = [None] * n_chains
        for z_t, lk_t, both_t, vt in zip(zs, log_keeps, boths, vts):
            a_s = [jnp.exp(z + lk + (both[:, :n_rows] + both[:, n_rows:] + run))
                   for z, lk, both, run in zip(z_t, lk_t, both_t, runs)]
            if diagonal:
                a_s = [jnp.where(m, a, 0.0) for m, a in zip(masks, a_s)]
            pvs = [_dot(vt, a.astype(BF16)) for a in a_s]
            totals = [pv if tot is None else tot + pv for tot, pv in zip(totals, pvs)]
            runs = [run + jnp.sum(lk, axis=0, keepdims=True) for run, lk in zip(runs, lk_t)]
        for c in chains:
            acc_ref[c] += totals[c]
            run_ref[c] = runs[c]

    step([i], True)

    def body(jj, carry):
        j = i - 1 - 2 * jj
        step([j, j - 1], False)
        return carry

    lax.fori_loop(0, i // 2, body, 0)

    @pl.when(i % 2 == 1)
    def _():
        step([0], False)

    for c in range(n_chains):
        acc = acc_ref[c]
        out_t = jnp.concatenate([acc[h * HEAD_DIM:(h + 1) * HEAD_DIM, h * tc:(h + 1) * tc] for h in range(SB_HEADS)],
                                axis=0)
        o_ref[c * tc:(c + 1) * tc, :] = out_t.T * _silu(g_ref[c * tc:(c + 1) * tc, :])


def _sb_prompt(sbqt, sbkv, sbkv_all, sbg, layer, batch, seq):
    tq = K_TILE
    n_chains = K_TILE // Q_TILE
    nq = seq // tq
    return pl.pallas_call(
        _sb_prompt_kernel,
        grid=(batch, nq),
        in_specs=[pl.BlockSpec((256, tq), lambda b, i: (0, b * nq + i)),
                  pl.BlockSpec((seq, 256), lambda b, i: (b, 0)),
                  pl.BlockSpec((1, 1, 256, seq), lambda b, i: (layer, b, 1, 0)),
                  pl.BlockSpec((tq, 256), lambda b, i: (b * nq + i, 0))],
        out_specs=pl.BlockSpec((tq, 256), lambda b, i: (b * nq + i, 0)),
        out_shape=jax.ShapeDtypeStruct((batch * seq, 256), F32),
        scratch_shapes=[pltpu.VMEM((n_chains, SB_WIDTH, SB_HEADS * Q_TILE), F32),
                        pltpu.VMEM((n_chains, 1, SB_HEADS * Q_TILE), F32)],
        compiler_params=pltpu.CompilerParams(dimension_semantics=("parallel", "parallel"),
                                             vmem_limit_bytes=VMEM_LIMIT),
        name="sb_prompt",
    )(sbqt, sbkv, sbkv_all, sbg)


def _stack_sb_heads(q):
    head = lax.broadcasted_iota(jnp.int32, q.shape, 1) // HEAD_DIM
    return jnp.concatenate([jnp.where(head == h, q, 0.0) for h in range(SB_HEADS)], axis=0).astype(BF16)


def _unstack_sb_heads(acc, t):
    head = lax.broadcasted_iota(jnp.int32, (t, SB_WIDTH), 1) // HEAD_DIM
    out = jnp.zeros((t, SB_WIDTH), F32)
    for h in range(SB_HEADS):
        out = jnp.where(head == h, acc[h * t:(h + 1) * t], out)
    return out


def _side_by_side(make_body, n):
    live = [make_body(k) for k in range(n)]
    while live:
        still = []
        for gen in live:
            try:
                next(gen)
                still.append(gen)
            except StopIteration:
                pass
        live = still


def _sb_sample_kernel(pt_ref, q_ref, kvn_ref, g_ref, *rest, n_pages, page, group):
    all_pages = rest[:n_pages * group]
    o_ref = rest[n_pages * group]
    new_ref = rest[n_pages * group + 1]
    t = q_ref.shape[0] // group
    rows = SB_HEADS * t
    u = _later_matrix(page)
    trow = lax.broadcasted_iota(jnp.int32, (rows, page), 0) % t
    scol = lax.broadcasted_iota(jnp.int32, (rows, page), 1)
    mask = scol < trow
    order = list(reversed(range(n_pages)))

    def body(k):
        rs = slice(k * t, (k + 1) * t)
        pages = all_pages[k * n_pages:(k + 1) * n_pages]
        q4 = _stack_sb_heads(q_ref[rs, :] * SCALE)
        new_ref[k] = jnp.zeros(new_ref.shape[1:], F32)
        new_ref[k, 0:t, :] = kvn_ref[rs, :]
        z_new = _dot_nt(q4, new_ref[k, :, 0:256].astype(BF16))
        yield
        zs = [z_new] + [_dot(q4, pages[p][0, 0, 0:256, :].astype(BF16)) for p in order]
        yield
        log_keeps = [jnp.where(mask, -_softplus(z_new), 0.0)] + [-_softplus(z) for z in zs[1:]]
        halves = []
        for lk in log_keeps:
            halves += list(_split_bf16(lk))
        yield
        both = _dot(jnp.concatenate(halves, axis=0), u)
        yield
        run = jnp.zeros((rows, 1), F32)
        acc = jnp.zeros((rows, SB_WIDTH), F32)
        for idx, (z, lk) in enumerate(zip(zs, log_keeps)):
            later = both[2 * idx * rows:(2 * idx + 1) * rows] + both[(2 * idx + 1) * rows:(2 * idx + 2) * rows] + run
            a = jnp.exp(z + lk + later)
            if idx == 0:
                acc = acc + _dot(jnp.where(mask, a, 0.0).astype(BF16), new_ref[k, :, 256:512].astype(BF16))
            else:
                acc = acc + _dot_nt(a.astype(BF16), pages[order[idx - 1]][0, 0, 256:512, :].astype(BF16))
            run = run + jnp.sum(lk, axis=1, keepdims=True)
        yield
        o_ref[rs, :] = _unstack_sb_heads(acc, t) * _silu(g_ref[rs, :])

    _side_by_side(body, group)


def _page_specs(layer, n_pages, page, group):
    def spec(k, p):
        return pl.BlockSpec((1, 1, 512, page), lambda b, pt: (layer, pt[b * group + k, p], 0, 0))
    return [spec(k, p) for k in range(group) for p in range(n_pages)]


def _sb_sample(page_table, sbq, sbkv, sbg, cache_t, layer, row0, dec_batch, t):
    n_pages = page_table.shape[1]
    page = cache_t.shape[3]
    group = SAMPLE_GROUP
    gt = group * t
    blk0 = row0 // gt
    own = lambda w: pl.BlockSpec((gt, w), lambda b, pt: (b, 0))
    joint = lambda w: pl.BlockSpec((gt, w), lambda b, pt: (blk0 + b, 0))
    grid_spec = pltpu.PrefetchScalarGridSpec(
        num_scalar_prefetch=1,
        grid=(dec_batch // group,),
        in_specs=[own(256), joint(512), joint(256)] + _page_specs(layer, n_pages, page, group),
        out_specs=pl.BlockSpec((gt, 256), lambda b, pt: (b, 0)),
        scratch_shapes=[pltpu.VMEM((group, page, 512), F32)],
    )
    return pl.pallas_call(
        functools.partial(_sb_sample_kernel, n_pages=n_pages, page=page, group=group),
        grid_spec=grid_spec,
        out_shape=jax.ShapeDtypeStruct((dec_batch * t, 256), F32),
        compiler_params=pltpu.CompilerParams(dimension_semantics=("parallel",), vmem_limit_bytes=VMEM_LIMIT),
        name="sb_sample",
    )(page_table, sbq, sbkv, sbg, *([cache_t] * (n_pages * group)))


def _pool_kernel(u_ref, g_ref, w_ref, sc_ref, *rest, nb, t, pos0, has_prefix):
    if has_prefix:
        pre_ref, o_ref, rows_ref = rest
    else:
        o_ref, rows_ref = rest
    pad = POOL_BUF + 1
    u = u_ref[...].reshape(nb, t, POOL_WIDTH)
    rows_ref[:, 0:pad, :] = jnp.zeros((nb, pad, POOL_WIDTH), F32)
    if has_prefix:
        rows_ref[:, 1:pad, :] = pre_ref[0]
    rows_ref[:, pad:pad + t, :] = u
    back = lambda s: rows_ref[:, pad - s:pad - s + t, :]
    sums = {}
    total = u
    done = 1
    for w in POOL_WINDOWS:
        for s in range(done, w):
            total = total + back(s)
        done = w
        sums[w] = total
    lane = lax.broadcasted_iota(jnp.int32, (nb, t, POOL_WIDTH), 2) // POOL_GDIM
    pos = pos0 + lax.broadcasted_iota(jnp.int32, (nb, t, POOL_WIDTH), 1)
    window = jnp.zeros((nb, t, POOL_WIDTH), jnp.int32)
    picked = jnp.zeros((nb, t, POOL_WIDTH), F32)
    for gi, w in enumerate(POOL_WINDOWS):
        window = jnp.where(lane == gi, w, window)
        picked = jnp.where(lane == gi, sums[w], picked)
    count = jnp.minimum(pos + 1, window).astype(F32)
    d = picked / count - u
    y = _dot(d.reshape(nb * t, POOL_WIDTH).astype(BF16), w_ref[...]) * sc_ref[...]
    o_ref[...] = y * _silu(g_ref[...])


def _pool(pu, pg, w_bd, scale, prefix, layer, row0, nb, t, n_seq, pos0):
    rows = nb * t
    blk0 = row0 // rows
    has_prefix = prefix is not None
    in_specs = [pl.BlockSpec((rows, 256), lambda i: (blk0 + i, 0)),
                pl.BlockSpec((rows, 256), lambda i: (blk0 + i, 0)),
                pl.BlockSpec((256, 256), lambda i: (0, 0)),
                pl.BlockSpec((1, 256), lambda i: (0, 0))]
    args = [pu, pg, w_bd, scale]
    if has_prefix:
        in_specs.append(pl.BlockSpec((1, nb, POOL_BUF, 256), lambda i: (layer, i, 0, 0)))
        args.append(prefix)
    return pl.pallas_call(
        functools.partial(_pool_kernel, nb=nb, t=t, pos0=pos0, has_prefix=has_prefix),
        grid=(n_seq // nb,),
        in_specs=in_specs,
        out_specs=pl.BlockSpec((rows, 256), lambda i: (i, 0)),
        out_shape=jax.ShapeDtypeStruct((n_seq * t, 256), F32),
        scratch_shapes=[pltpu.VMEM((nb, POOL_BUF + 1 + t, 256), F32)],
        compiler_params=pltpu.CompilerParams(dimension_semantics=("parallel",), vmem_limit_bytes=VMEM_LIMIT),
        name="pool_sample" if has_prefix else "pool_prompt",
    )(*args)


def _stack_nsa_heads_t(ref, cols, scale):
    t = cols.stop - cols.start
    low = lax.broadcasted_iota(jnp.int32, (LANES, t), 0) < HEAD_DIM
    g0, g1 = [], []
    for c in range(NSA_GROUP):
        ch = ref[c * LANES:(c + 1) * LANES, cols] * scale
        g0.append(jnp.where(low, ch, 0.0))
        g1.append(jnp.where(low, 0.0, ch))
    return jnp.concatenate(g0 + g1, axis=1).astype(BF16)


def _per_group_lanes(x, t):
    a, b = x[:, :t], x[:, t:]
    return jnp.concatenate([a] * NSA_GROUP + [b] * NSA_GROUP, axis=1)


def _select_blocks_t(imp, qpos, n_sel):
    blk = lax.broadcasted_iota(jnp.int32, imp.shape, 0)
    forced = (blk == qpos // SEL_BLOCK) | (blk == 0)
    future = blk * SEL_BLOCK > qpos
    imp = jnp.where(forced, FORCED_SCORE, jnp.where(future, -1.0, imp))
    rank = jnp.zeros(imp.shape, jnp.int32)
    for i in range(n_sel):
        row = imp[i:i + 1, :]
        ahead = (row > imp) | ((row == imp) & (blk > i))
        rank = rank + ahead.astype(jnp.int32)
    return (rank < min(SEL_TOPN, n_sel)).astype(F32)


def _nsa_prompt_kernel(nqct_ref, nqrt_ref, bgt_ref, ng_ref, cmp_ref, ks_ref, vst_ref, wk_ref, wvt_ref,
                       o_ref, m_ref, l_ref, acc_ref, oc_ref, os_ref, ow_ref):
    tq = nqct_ref.shape[1]
    tk = tq
    n_chains = acc_ref.shape[0]
    tc = tq // n_chains
    chains = range(n_chains)
    n_sel = cmp_ref.shape[0]
    i = pl.program_id(1)
    q0 = i * tq
    n_rows = NSA_HEADS * tc
    pos_in = lax.broadcasted_iota(jnp.int32, (1, n_rows), 1) % tc
    pos2_in = lax.broadcasted_iota(jnp.int32, (1, NSA_KV_HEADS * tc), 1) % tc
    krow = lax.broadcasted_iota(jnp.int32, (tk, n_rows), 0)
    blk = lax.broadcasted_iota(jnp.int32, (n_sel, n_rows), 0)

    qrts, q_augs = [None] * n_chains, [None] * n_chains

    def select_body(c):
        cols = slice(c * tc, (c + 1) * tc)
        qct = _stack_nsa_heads_t(nqct_ref, cols, SCALE)
        qrt = _stack_nsa_heads_t(nqrt_ref, cols, SCALE * LOG2E)
        qpos = q0 + c * tc + pos_in
        yield
        s_e = _dot(cmp_ref[:, 0:128].astype(BF16), qct)
        s_o = _dot(cmp_ref[:, 128:256].astype(BF16), qct)
        yield
        s_e = jnp.where(blk * SEL_BLOCK + (CMP_BLOCK - 1) <= qpos, s_e, NEG_INF)
        s_o = jnp.where(blk * SEL_BLOCK + (SEL_BLOCK - 1) <= qpos, s_o, NEG_INF)
        m = jnp.maximum(jnp.max(s_e, axis=0, keepdims=True), jnp.max(s_o, axis=0, keepdims=True))
        m = jnp.where(m == NEG_INF, 0.0, m)
        p_e = jnp.exp(s_e - m)
        p_o = jnp.exp(s_o - m)
        denom = jnp.maximum(jnp.sum(p_e, axis=0, keepdims=True) + jnp.sum(p_o, axis=0, keepdims=True), 1e-30)
        p_e = p_e / denom
        p_o = p_o / denom
        yield
        oc_ref[c] = (_dot_tn(cmp_ref[:, 256:384].astype(BF16), p_e.astype(BF16))
                     + _dot_tn(cmp_ref[:, 384:512].astype(BF16), p_o.astype(BF16)))
        pair = p_e + p_o
        imp = []
        for g in range(NSA_KV_HEADS):
            acc = pair[:, (g * NSA_GROUP) * tc:(g * NSA_GROUP + 1) * tc]
            for h in range(1, NSA_GROUP):
                acc = acc + pair[:, (g * NSA_GROUP + h) * tc:(g * NSA_GROUP + h + 1) * tc]
            imp.append(acc)
        yield
        sel = _select_blocks_t(jnp.concatenate(imp, axis=1), q0 + c * tc + pos2_in, n_sel)
        member_bias = _per_group_lanes((sel - 1.0) * (-MASKED), tc).astype(BF16)
        qrts[c] = qrt
        q_augs[c] = jnp.concatenate([qrt, member_bias], axis=0)

    _side_by_side(select_body, n_chains)

    def reset():
        m_ref[...] = jnp.full_like(m_ref, NEG_INF)
        l_ref[...] = jnp.zeros_like(l_ref)
        acc_ref[...] = jnp.zeros_like(acc_ref)

    def online_softmax_phases(scores, vt):
        m_olds = [m_ref[c] for c in chains]
        m_news = [jnp.maximum(m_old, jnp.max(s, axis=0, keepdims=True)) for m_old, s in zip(m_olds, scores)]
        ps = [jnp.exp2(s - m_new) for s, m_new in zip(scores, m_news)]
        yield
        pvs = [_dot(vt, p.astype(BF16)) for p in ps]
        yield
        for c in chains:
            alpha = jnp.exp2(m_olds[c] - m_news[c])
            l_ref[c] = alpha * l_ref[c] + jnp.sum(ps[c], axis=0, keepdims=True)
            acc_ref[c] = alpha * acc_ref[c] + pvs[c]
            m_ref[c] = m_news[c]

    def online_softmax(scores, vt):
        for _ in online_softmax_phases(scores, vt):
            pass

    def finish(dst_ref):
        for c in chains:
            dst_ref[c] = acc_ref[c] / jnp.maximum(l_ref[c], 1e-30)

    reset()

    def slc_phases(j, diagonal):
        start = pl.multiple_of(j * tk, tk)
        key_blk = (j * tk + lax.broadcasted_iota(jnp.int32, (tk, n_sel), 0)) // SEL_BLOCK
        onehot = (key_blk == lax.broadcasted_iota(jnp.int32, (tk, n_sel), 1)).astype(BF16)
        k_aug = jnp.concatenate([ks_ref[pl.ds(start, tk), :].astype(BF16), onehot], axis=1)
        scores = [_dot(k_aug, q_aug) for q_aug in q_augs]
        if diagonal:
            scores = [jnp.where(krow <= pos_in + c * tc, s, MASKED) for c, s in zip(chains, scores)]
        yield
        yield from online_softmax_phases(scores, vst_ref[0, 0, :, pl.ds(start, tk)].astype(BF16))

    def slc_body(j, carry):
        for _ in slc_phases(j, False):
            pass
        return carry

    def slc_diagonal():
        yield from slc_phases(i, True)
        finish(os_ref)

    lax.fori_loop(0, i, slc_body, 0)

    span = WINDOW + tc

    def window_slab():
        kc = lax.broadcasted_iota(jnp.int32, (span, tc), 0)
        qr = lax.broadcasted_iota(jnp.int32, (span, tc), 1)
        bias = jnp.where((kc > qr) & (kc <= qr + WINDOW), 0.0, MASKED)
        bias = jnp.concatenate([bias] * NSA_HEADS, axis=1)
        starts = [pl.multiple_of(q0 + c * tc - WINDOW, tc) for c in chains]
        scores = [_dot(wk_ref[pl.ds(start, span), :].astype(BF16), qrt) + bias for start, qrt in zip(starts, qrts)]
        yield
        ps = [jnp.exp2(s - jnp.max(s, axis=0, keepdims=True)) for s in scores]
        yield
        pvs = [_dot(wvt_ref[0, :, pl.ds(start, span)].astype(BF16), p.astype(BF16)) for start, p in zip(starts, ps)]
        yield
        for c in chains:
            ow_ref[c] = pvs[c] / jnp.sum(ps[c], axis=0, keepdims=True)

    @pl.when(q0 >= WINDOW)
    def _():
        _side_by_side(lambda which: (slc_diagonal, window_slab)[which](), 2)

    @pl.when(q0 < WINDOW)
    def _():
        for _ in slc_diagonal():
            pass
        reset()

        def win_body(j, carry):
            start = pl.multiple_of(j * tk, tk)
            k = wk_ref[pl.ds(start, tk), :].astype(BF16)
            dists = [(q0 + c * tc + pos_in) - (krow + j * tk) for c in chains]
            scores = [jnp.where((dist >= 0) & (dist < WINDOW), _dot(k, qrt), MASKED) for dist, qrt in zip(dists, qrts)]
            online_softmax(scores, wvt_ref[0, :, pl.ds(start, tk)].astype(BF16))
            return carry

        lax.fori_loop(0, i + 1, win_body, 0)
        finish(ow_ref)

    low = lax.broadcasted_iota(jnp.int32, (LANES, tc), 0) < HEAD_DIM
    for c in chains:
        rows = slice(c * tc, (c + 1) * tc)
        gate = 1.0 / (1.0 + jnp.exp(-bgt_ref[:, rows]))
        o_c, o_s, o_w = oc_ref[c], os_ref[c], ow_ref[c]
        for ch in range(NSA_GROUP):
            per_group = []
            for g in range(NSA_KV_HEADS):
                hd = g * NSA_GROUP + ch
                sl = slice(hd * tc, (hd + 1) * tc)
                per_group.append(gate[3 * hd:3 * hd + 1, :] * o_c[:, sl] + gate[3 * hd + 1:3 * hd + 2, :] * o_s[:, sl]
                                 + gate[3 * hd + 2:3 * hd + 3, :] * o_w[:, sl])
            mixed = jnp.where(low, per_group[0], per_group[1])
            cs = slice(ch * LANES, (ch + 1) * LANES)
            o_ref[rows, cs] = mixed.T * _silu(ng_ref[rows, cs])


def _nsa_prompt(nqct, nqrt, bgt, ngate, cmp, nsa_new, nsa_all, win_new, win_t, layer, batch, seq):
    tq = NSA_TILE
    n_chains = NSA_TILE // Q_TILE
    nq = seq // tq
    n_sel = seq // SEL_BLOCK
    qcol = lambda f: pl.BlockSpec((f, tq), lambda b, i: (0, b * nq + i))
    qrow = lambda w: pl.BlockSpec((tq, w), lambda b, i: (b * nq + i, 0))
    n_rows = NSA_HEADS * Q_TILE
    stat = pltpu.VMEM((n_chains, 1, n_rows), F32)
    wide = pltpu.VMEM((n_chains, LANES, n_rows), F32)
    return pl.pallas_call(
        _nsa_prompt_kernel,
        grid=(batch, nq),
        in_specs=[qcol(512), qcol(512), qcol(128), qrow(512),
                  pl.BlockSpec((n_sel, 512), lambda b, i: (b, 0)),
                  pl.BlockSpec((seq, LANES), lambda b, i: (b, 2)),
                  pl.BlockSpec((1, 1, LANES, seq), lambda b, i: (layer, b, 3, 0)),
                  pl.BlockSpec((seq, LANES), lambda b, i: (b, 0)),
                  pl.BlockSpec((1, LANES, seq), lambda b, i: (b, 1, 0))],
        out_specs=qrow(512),
        out_shape=jax.ShapeDtypeStruct((batch * seq, 512), F32),
        scratch_shapes=[stat, stat, wide, wide, wide, wide],
        compiler_params=pltpu.CompilerParams(dimension_semantics=("parallel", "parallel"),
                                             vmem_limit_bytes=VMEM_LIMIT),
        name="nsa_prompt",
    )(nqct, nqrt, bgt, ngate, cmp, nsa_new, nsa_all, win_new, win_t)


def _stack_nsa_heads(ref, rows):
    low = lax.broadcasted_iota(jnp.int32, (rows.stop - rows.start, LANES), 1) < HEAD_DIM
    g0, g1 = [], []
    for c in range(NSA_GROUP):
        ch = ref[rows, c * LANES:(c + 1) * LANES] * SCALE
        g0.append(jnp.where(low, ch, 0.0))
        g1.append(jnp.where(low, 0.0, ch))
    return jnp.concatenate(g0 + g1, axis=0).astype(BF16)


def _per_group_rows(x, t):
    a, b = x[:t], x[t:]
    return jnp.concatenate([a] * NSA_GROUP + [b] * NSA_GROUP, axis=0)


def _select_blocks(imp, qpos, n_sel):
    blk = lax.broadcasted_iota(jnp.int32, imp.shape, 1)
    forced = (blk == qpos // SEL_BLOCK) | (blk == 0)
    future = blk * SEL_BLOCK > qpos
    imp = jnp.where(forced, FORCED_SCORE, jnp.where(future, -1.0, imp))
    rank = jnp.zeros(imp.shape, jnp.int32)
    for i in range(n_sel):
        col = imp[:, i:i + 1]
        ahead = (col > imp) | ((col == imp) & (blk > i))
        rank = rank + ahead.astype(jnp.int32)
    return ((rank < min(SEL_TOPN, n_sel)) & (blk < n_sel)).astype(BF16)


def _expand_matrix(n_rows, n_keys, key0):
    r = lax.broadcasted_iota(jnp.int32, (n_rows, n_keys), 0)
    c = lax.broadcasted_iota(jnp.int32, (n_rows, n_keys), 1)
    return ((key0 + c) // SEL_BLOCK == r).astype(BF16)


def _softmax_parts(parts):
    m = None
    for s, _, _ in parts:
        pm = jnp.max(s, axis=1, keepdims=True)
        m = pm if m is None else jnp.maximum(m, pm)
    m = jnp.where(m == NEG_INF, 0.0, m)
    denom = None
    acc = None
    for s, v, feature_major in parts:
        p = jnp.exp(s - m)
        ps = jnp.sum(p, axis=1, keepdims=True)
        pv = _dot_nt(p.astype(BF16), v) if feature_major else _dot(p.astype(BF16), v)
        denom = ps if denom is None else denom + ps
        acc = pv if acc is None else acc + pv
    return acc / jnp.maximum(denom, 1e-30)


def _nsa_sample_kernel(pt_ref, nqc_ref, nqr_ref, new_ref, winn_ref, bg_ref, ng_ref, swin_ref, phi_ref, seg_ref,
                       *rest, n_pages, page, pos0, group):
    all_pages = rest[:n_pages * group]
    o_ref, ckall_ref, cvall_ref, newpad_ref, winpad_ref = rest[n_pages * group:]
    t = nqc_ref.shape[0] // group
    rows = NSA_HEADS * t
    n_sel = (pos0 + t + SEL_BLOCK - 1) // SEL_BLOCK
    wlen = swin_ref.shape[3]
    trow = lax.broadcasted_iota(jnp.int32, (rows, 1), 0) % t
    trow2 = lax.broadcasted_iota(jnp.int32, (NSA_KV_HEADS * t, 1), 0) % t
    qpos, qpos2 = pos0 + trow, pos0 + trow2
    col = lax.broadcasted_iota(jnp.int32, (rows, LANES), 1)
    block_end = jnp.where(col < SEL_BLOCK, col * SEL_BLOCK + (CMP_BLOCK - 1),
                          (col - SEL_BLOCK) * SEL_BLOCK + (SEL_BLOCK - 1))
    scol = lax.broadcasted_iota(jnp.int32, (NSA_KV_HEADS * t, page), 1)
    rcol = lax.broadcasted_iota(jnp.int32, (NSA_KV_HEADS * t, wlen), 1)
    dist = wlen + trow2 - rcol
    bias_w = _per_group_rows(jnp.where((dist >= 0) & (dist < WINDOW), 0.0, NEG_INF), t)
    dist_n = trow2 - scol
    bias_wn = _per_group_rows(jnp.where((dist_n >= 0) & (dist_n < WINDOW) & (scol < t), 0.0, NEG_INF), t)
    low = lax.broadcasted_iota(jnp.int32, (t, LANES), 1) < HEAD_DIM

    def body(k):
        rs = slice(k * t, (k + 1) * t)
        pages = all_pages[k * n_pages:(k + 1) * n_pages]
        qc = _stack_nsa_heads(nqc_ref, rs)
        qr = _stack_nsa_heads(nqr_ref, rs)
        for p in range(n_pages):
            ckall_ref[k, :, p * page:(p + 1) * page] = (pages[p][0, 0, 0:128, :] * phi_ref[0:1, :]).astype(BF16)
            cvall_ref[k, :, p * page:(p + 1) * page] = (pages[p][0, 0, 128:256, :] * phi_ref[1:2, :]).astype(BF16)
        newpad_ref[k] = jnp.zeros(newpad_ref.shape[1:], F32)
        newpad_ref[k, 0:t, :] = new_ref[rs, :]
        winpad_ref[k] = jnp.zeros(winpad_ref.shape[1:], F32)
        winpad_ref[k, 0:t, :] = winn_ref[rs, :]
        yield
        kcmp_t = _dot(ckall_ref[k], seg_ref[...]).astype(BF16)
        vcmp_t = _dot(cvall_ref[k], seg_ref[...]).astype(BF16)
        yield
        s = jnp.where(block_end <= qpos, _dot(qc, kcmp_t), NEG_INF)
        m = jnp.max(s, axis=1, keepdims=True)
        m = jnp.where(m == NEG_INF, 0.0, m)
        p_c = jnp.exp(s - m)
        p_c = p_c / jnp.maximum(jnp.sum(p_c, axis=1, keepdims=True), 1e-30)
        yield
        o_c = _dot_nt(p_c.astype(BF16), vcmp_t)
        imp = []
        for g in range(NSA_KV_HEADS):
            acc = p_c[(g * NSA_GROUP) * t:(g * NSA_GROUP + 1) * t]
            for c in range(1, NSA_GROUP):
                acc = acc + p_c[(g * NSA_GROUP + c) * t:(g * NSA_GROUP + c + 1) * t]
            imp.append(acc)
        imp = jnp.concatenate(imp, axis=0)
        imp = imp + pltpu.roll(imp, SEL_BLOCK, 1)
        sel = _select_blocks(imp, qpos2, n_sel)
        yield
        member = _dot(sel, _expand_matrix(LANES, n_pages * page, 0))
        member_new = _dot(sel, _expand_matrix(LANES, page, n_pages * page))
        raw = [_dot(qr, pages[p][0, 0, 256:384, :].astype(BF16)) for p in range(n_pages)]
        raw_new = _dot_nt(qr, newpad_ref[k, :, 256:384].astype(BF16))
        raw_w = _dot(qr, swin_ref[0, k, 0:128, :].astype(BF16))
        raw_wn = _dot_nt(qr, winpad_ref[k, :, 0:128].astype(BF16))
        yield
        bias_past = _per_group_rows(jnp.where(member > 0.5, 0.0, NEG_INF), t)
        ok_new = (member_new > 0.5) & (scol <= trow2) & (scol < t)
        bias_new = _per_group_rows(jnp.where(ok_new, 0.0, NEG_INF), t)
        parts = [(raw[p] + bias_past[:, p * page:(p + 1) * page], pages[p][0, 0, 384:512, :].astype(BF16), True)
                 for p in range(n_pages)]
        parts.append((raw_new + bias_new, newpad_ref[k, :, 384:512].astype(BF16), False))
        o_s = _softmax_parts(parts)
        yield
        o_w = _softmax_parts([(raw_w + bias_w, swin_ref[0, k, 128:256, :].astype(BF16), True),
                              (raw_wn + bias_wn, winpad_ref[k, :, 128:256].astype(BF16), False)])
        yield
        gate = 1.0 / (1.0 + jnp.exp(-bg_ref[rs, :]))
        for c in range(NSA_GROUP):
            per_group = []
            for g in range(NSA_KV_HEADS):
                r0 = (g * NSA_GROUP + c) * t
                cg = (g * NSA_GROUP + c) * 3
                per_group.append(gate[:, cg:cg + 1] * o_c[r0:r0 + t] + gate[:, cg + 1:cg + 2] * o_s[r0:r0 + t]
                                 + gate[:, cg + 2:cg + 3] * o_w[r0:r0 + t])
            mixed = jnp.where(low, per_group[0], per_group[1])
            sl = slice(c * LANES, (c + 1) * LANES)
            o_ref[rs, sl] = mixed * _silu(ng_ref[rs, sl])

    _side_by_side(body, group)


def _nsa_sample(page_table, nqc, nqr, nsa_new, win_new, bgate, ngate, state_win_t, phi_rows, seg, cache_t,
                layer, row0, dec_batch, t, pos0):
    n_pages = page_table.shape[1]
    page = cache_t.shape[3]
    group = SAMPLE_GROUP
    gt = group * t
    blk0 = row0 // gt
    own = lambda w: pl.BlockSpec((gt, w), lambda b, pt: (b, 0))
    joint = lambda w: pl.BlockSpec((gt, w), lambda b, pt: (blk0 + b, 0))
    wlen = state_win_t.shape[3]
    n_keys = n_pages * page
    grid_spec = pltpu.PrefetchScalarGridSpec(
        num_scalar_prefetch=1,
        grid=(dec_batch // group,),
        in_specs=[own(512), own(512), joint(512), joint(256), own(128), joint(512),
                  pl.BlockSpec((1, group, 256, wlen), lambda b, pt: (layer, b, 0, 0)),
                  pl.BlockSpec((2, page), lambda b, pt: (0, 0)),
                  pl.BlockSpec((n_keys, LANES), lambda b, pt: (0, 0))]
        + _page_specs(layer, n_pages, page, group),
        out_specs=pl.BlockSpec((gt, 512), lambda b, pt: (b, 0)),
        scratch_shapes=[pltpu.VMEM((group, LANES, n_keys), BF16), pltpu.VMEM((group, LANES, n_keys), BF16),
                        pltpu.VMEM((group, page, 512), F32), pltpu.VMEM((group, page, 256), F32)],
    )
    return pl.pallas_call(
        functools.partial(_nsa_sample_kernel, n_pages=n_pages, page=page, pos0=pos0, group=group),
        grid_spec=grid_spec,
        out_shape=jax.ShapeDtypeStruct((dec_batch * t, 512), F32),
        compiler_params=pltpu.CompilerParams(dimension_semantics=("parallel",), vmem_limit_bytes=VMEM_LIMIT),
        name="nsa_sample",
    )(page_table, nqc, nqr, nsa_new, win_new, bgate, ngate, state_win_t, phi_rows, seg,
      *([cache_t] * (n_pages * group)))


def _out_proj_kernel(oap_ref, obp_ref, ocp_ref, oas_ref, obs_ref, ocs_ref, x_ref, w_ref, g_ref, y_ref, *,
                     n_prompt_tiles):
    prompt = pl.program_id(0) < n_prompt_tiles
    pick = lambda p_ref, s_ref: jnp.where(prompt, p_ref[...], s_ref[...]).astype(BF16)
    out = (_dot(pick(oap_ref, oas_ref), w_ref[0:256, :])
           + _dot(pick(obp_ref, obs_ref), w_ref[256:512, :])
           + _dot(pick(ocp_ref, ocs_ref), w_ref[512:1024, :]))
    ms = jnp.mean(out * out, axis=-1, keepdims=True)
    y_ref[...] = x_ref[...] + out * lax.rsqrt(ms + NORM_EPS) * g_ref[...]


def _out_proj(prompt_parts, sample_parts, x, w_out_p, g_post, n_p):
    n = x.shape[0]
    tm = ROW_TILE
    npt = n_p // tm
    row = lambda w: pl.BlockSpec((tm, w), lambda i: (i, 0))
    prompt_row = lambda w: pl.BlockSpec((tm, w), lambda i: (jnp.minimum(i, npt - 1), 0))
    sample_row = lambda w: pl.BlockSpec((tm, w), lambda i: (jnp.maximum(i - npt, 0), 0))
    fixed = lambda s: pl.BlockSpec(s, lambda i: (0, 0), pipeline_mode=pl.Buffered(1))
    widths = (256, 256, 512)
    return pl.pallas_call(
        functools.partial(_out_proj_kernel, n_prompt_tiles=npt),
        grid=(n // tm,),
        in_specs=[prompt_row(w) for w in widths] + [sample_row(w) for w in widths]
        + [row(D_MODEL), fixed((D_MODEL, D_MODEL)), fixed((1, D_MODEL))],
        out_specs=row(D_MODEL),
        out_shape=jax.ShapeDtypeStruct((n, D_MODEL), F32),
        compiler_params=pltpu.CompilerParams(dimension_semantics=("arbitrary",), vmem_limit_bytes=VMEM_LIMIT),
        name="out_proj",
    )(*prompt_parts, *sample_parts, x, w_out_p, g_post)


def _win_update_kernel(state_ref, new_ref, o_ref, *, nb, t):
    wlen = state_ref.shape[3]
    for bi in range(nb):
        o_ref[0, bi] = pltpu.roll(state_ref[0, bi], wlen - t, 1)
        o_ref[0, bi, :, wlen - t:wlen] = new_ref[0, bi * t:(bi + 1) * t, :].T


def _win_update(state_win_t, win_rows, t):
    depth, bs, width, wlen = state_win_t.shape
    nb = 8
    return pl.pallas_call(
        functools.partial(_win_update_kernel, nb=nb, t=t),
        grid=(depth, bs // nb),
        in_specs=[pl.BlockSpec((1, nb, width, wlen), lambda l, b: (l, b, 0, 0)),
                  pl.BlockSpec((1, nb * t, width), lambda l, b: (l, b, 0))],
        out_specs=pl.BlockSpec((1, nb, width, wlen), lambda l, b: (l, b, 0, 0)),
        out_shape=jax.ShapeDtypeStruct(state_win_t.shape, F32),
        compiler_params=pltpu.CompilerParams(dimension_semantics=("parallel", "parallel"),
                                             vmem_limit_bytes=VMEM_LIMIT),
        name="win_update",
    )(state_win_t, win_rows)


def _to_chunk_layout(w, axis):
    shape = w.shape
    split = shape[:axis] + (NSA_KV_HEADS, NSA_GROUP, HEAD_DIM) + shape[axis + 1:]
    return jnp.swapaxes(w.reshape(split), axis, axis + 1).reshape(shape)


def _rope_tables(pos):
    inv = ROPE_THETA ** (-jnp.arange(ROT_HALF, dtype=F32) * (2.0 / ROT_DIM))
    ang = pos.astype(F32)[:, None] * inv[None, :]
    cos, sin = jnp.cos(ang), jnp.sin(ang)
    n = pos.shape[0]
    cos_h = jnp.concatenate([cos, cos, jnp.ones((n, HEAD_DIM - ROT_DIM), F32)], axis=1)
    sin_h = jnp.concatenate([-sin, sin, jnp.zeros((n, HEAD_DIM - ROT_DIM), F32)], axis=1)
    reps = (1, LANES // HEAD_DIM)
    return jnp.tile(cos_h, reps), jnp.tile(sin_h, reps), cos.T, sin.T


def _segment_matrix(n_keys):
    n = np.arange(n_keys) // CMP_BLOCK
    col = (n % 2) * SEL_BLOCK + n // 2
    return jnp.asarray(col[:, None] == np.arange(LANES)[None, :], BF16)


def kernel(x_prompt, x_sample, cache_sb_kv, cache_nsa_kv, state_win_kv, state_pool, page_table, norm_pre, w_in,
           pool_w, pool_scale, phi_k, phi_v, w_out, norm_post):
    bp, tp, d = x_prompt.shape
    bs, ts, _ = x_sample.shape
    depth = w_in.shape[0]
    n_pool, page = cache_sb_kv.shape[1], cache_sb_kv.shape[2]
    past_len = page_table.shape[1] * page
    wlen = state_win_kv.shape[2]
    n_p, n_s = bp * tp, bs * ts
    assert d == D_MODEL and n_p % ROW_TILE == 0 and n_s % ROW_TILE == 0
    assert tp % K_TILE == 0 and K_TILE % Q_TILE == 0 and tp % SEL_BLOCK == 0 and n_p % n_s == 0
    assert tp % NSA_TILE == 0 and NSA_TILE % Q_TILE == 0
    assert past_len % SEL_BLOCK == 0 and ts < CMP_BLOCK and ts % 8 == 0 and page == LANES
    assert past_len // CMP_BLOCK <= SEL_BLOCK and (wlen >= WINDOW or wlen == past_len)

    o = np.cumsum((0,) + (256,) * 6 + (512,) + (128,) * 6 + (24, 512)).tolist()
    w_rows = jnp.swapaxes(w_in, 1, 2)
    w_t = jnp.concatenate([
        w_rows[:, o[0]:o[1]], _to_chunk_layout(w_rows[:, o[6]:o[7]], 1), w_rows[:, o[13]:o[14]],
        jnp.zeros((depth, IN_WIDTH_P - o[15], D_MODEL), w_in.dtype),
        w_rows[:, o[1]:o[6]], w_rows[:, o[7]:o[13]], _to_chunk_layout(w_rows[:, o[14]:o[15]], 1)],
        axis=1).astype(BF16)
    w_out_p = jnp.concatenate([w_out[:, :512], _to_chunk_layout(w_out[:, 512:], 1)], axis=1).astype(BF16)
    w_pool = jnp.zeros((depth, POOL_WIDTH, POOL_WIDTH), F32)
    for g in range(len(POOL_WINDOWS)):
        sl = slice(g * POOL_GDIM, (g + 1) * POOL_GDIM)
        w_pool = w_pool.at[:, sl, sl].set(pool_w[:, g])
    w_pool = w_pool.astype(BF16)
    reps = page // CMP_BLOCK
    phi_rows = jnp.stack([jnp.tile(phi_k, (1, reps)), jnp.tile(phi_v, (1, reps))], axis=1)
    seg = _segment_matrix(past_len)

    pos = jnp.concatenate([jnp.tile(jnp.arange(tp, dtype=jnp.int32), bp),
                           jnp.tile(past_len + jnp.arange(ts, dtype=jnp.int32), bs)])
    tables = _rope_tables(pos)

    cache_sb_t = jnp.transpose(cache_sb_kv, (0, 1, 3, 4, 5, 2)).reshape(depth, n_pool, 512, page)
    cache_nsa_t = jnp.transpose(cache_nsa_kv, (0, 1, 3, 4, 5, 2)).reshape(depth, n_pool, 512, page)
    state_win_t = jnp.transpose(state_win_kv, (0, 1, 3, 4, 5, 2)).reshape(depth, bs, 256, wlen)

    x = jnp.concatenate([x_prompt.reshape(n_p, d), x_sample.reshape(n_s, d)], axis=0)
    sbkv_all = jnp.zeros((depth, bp, 512, tp), F32)
    nsa_all = jnp.zeros((depth, bp, 512, tp), F32)
    keep_p = min(WINDOW, tp)
    win_p, pool_p, sb_s, nsa_s, win_rows, pool_s = [], [], [], [], [], []
    for layer in range(depth):
        (sbkv, sbg, pu, pg, nsa_new, win_new, ngate, cmp, sbq, nqc, nqr, bgate,
         sbqt, nqct, nqrt, bgt, sbkv_all, nsa_all, win_t) = _in_proj(
            x, norm_pre[layer][None], w_t[layer], phi_k[layer], phi_v[layer], tables, sbkv_all, nsa_all,
            layer, n_p, bp, tp)
        pool_args = (pu, pg, w_pool[layer], pool_scale[layer][None])
        prompt_parts = (
            _sb_prompt(sbqt, sbkv, sbkv_all, sbg, layer, bp, tp),
            _pool(*pool_args, None, layer, 0, 1, tp, bp, 0),
            _nsa_prompt(nqct, nqrt, bgt, ngate, cmp, nsa_new, nsa_all, win_new, win_t, layer, bp, tp))
        sample_parts = (
            _sb_sample(page_table, sbq, sbkv, sbg, cache_sb_t, layer, n_p, bs, ts),
            _pool(*pool_args, state_pool, layer, n_p, bs, ts, bs, past_len),
            _nsa_sample(page_table, nqc, nqr, nsa_new, win_new, bgate, ngate, state_win_t, phi_rows[layer], seg,
                        cache_nsa_t, layer, n_p, bs, ts, past_len))
        x = _out_proj(prompt_parts, sample_parts, x, w_out_p[layer], norm_post[layer][None], n_p)

        win_p.append(win_t[:, :, tp - keep_p:])
        pool_p.append(pu[:n_p].reshape(bp, tp, POOL_WIDTH)[:, tp - POOL_BUF:])
        sb_s.append(sbkv[n_p:].reshape(bs, ts, 2, SB_HEADS, HEAD_DIM))
        nsa_s.append(nsa_new[n_p:].reshape(bs, ts, 4, NSA_KV_HEADS, HEAD_DIM))
        win_rows.append(win_new[n_p:])
        pool_s.append(jnp.concatenate([state_pool[layer], pu[n_p:].reshape(bs, ts, 256)], axis=1)[:, -POOL_BUF:])
    win_s_t = _win_update(state_win_t, jnp.stack(win_rows), ts)

    def token_major(a, n_kv, n_heads):
        lead = a.shape[:-2]
        a = a.reshape(lead + (n_kv, n_heads, HEAD_DIM, a.shape[-1]))
        return jnp.moveaxis(a, -1, len(lead))

    y_prompt = x[:n_p].reshape(bp, tp, d)
    y_sample = x[n_p:].reshape(bs, ts, d)
    return (y_prompt, y_sample,
            token_major(sbkv_all, 2, SB_HEADS), token_major(nsa_all, 4, NSA_KV_HEADS),
            token_major(jnp.stack(win_p), 2, NSA_KV_HEADS), jnp.stack(pool_p),
            jnp.stack(sb_s), jnp.stack(nsa_s), token_major(win_s_t, 2, NSA_KV_HEADS), jnp.stack(pool_s))
```
